```python
import jax, jax.numpy as jnp
from jax import lax
import numpy as np

D_MODEL = 2048
BATCH = 1
SEQ = 8192
DEPTH = 4

N_META = 16
FRONT = 128
N_PAD = FRONT - N_META

N_A_LAYERS = DEPTH // 2
N_B_LAYERS = DEPTH - N_A_LAYERS

ALPHA = (2.0 * DEPTH) ** 0.25
BETA = (8.0 * DEPTH) ** -0.25
LN_EPS = 1e-5

GLA_HEADS = 4
GLA_DK = D_MODEL // 2
GLA_DV = D_MODEL
GLA_HK = GLA_DK // GLA_HEADS
GLA_HV = GLA_DV // GLA_HEADS
GLA_RANK = 16
GLA_TAU = 16.0
GLA_CHUNK = 64
GLA_IN = 2 * GLA_DK + GLA_DV + GLA_RANK + GLA_DV

FOX_HEADS = 16
FOX_HD = D_MODEL // FOX_HEADS
FOX_BLOCK = 128
KV_OUT = 2 * D_MODEL + FOX_HEADS

D_FF = 5632
CONV_W = 3

kernel_name = "gla_fox_yoco_deepnorm_hybrid"


def layer_norm(x, g, b):
    xf = x.astype(jnp.float32)
    mu = xf.mean(-1, keepdims=True)
    var = jnp.square(xf - mu).mean(-1, keepdims=True)
    return ((xf - mu) * lax.rsqrt(var + LN_EPS) * g + b).astype(x.dtype)


def rms_norm(x, g):
    xf = x.astype(jnp.float32)
    y = xf * lax.rsqrt(jnp.mean(jnp.square(xf), -1, keepdims=True) + LN_EPS)
    return (y * g).astype(x.dtype)


def gla_mixer(x, valid, w_in, w_g2, b_g2, norm_g, w_out):
    B, L, _ = x.shape
    n_chunks = L // GLA_CHUNK
    proj = x @ w_in
    q, k, v, g_low, r = jnp.split(
        proj, [GLA_DK, 2 * GLA_DK, 2 * GLA_DK + GLA_DV, 2 * GLA_DK + GLA_DV + GLA_RANK], axis=-1)
    m = valid[None, :, None]
    log_a = jax.nn.log_sigmoid((g_low @ w_g2 + b_g2).astype(jnp.float32)) / GLA_TAU
    log_a = jnp.where(m, log_a, 0.0)
    k = jnp.where(m, k, 0)
    q = q * (GLA_HK ** -0.5)

    def to_chunks(t, hd):
        return t.astype(jnp.float32).reshape(B, n_chunks, GLA_CHUNK, GLA_HEADS, hd).transpose(1, 0, 3, 2, 4)

    qc, kc, gc = to_chunks(q, GLA_HK), to_chunks(k, GLA_HK), to_chunks(log_a, GLA_HK)
    vc = to_chunks(v, GLA_HV)
    causal = jnp.tril(jnp.ones((GLA_CHUNK, GLA_CHUNK), dtype=bool))

    def step(S, inp):
        q_c, k_c, v_c, g_c = inp
        b = jnp.cumsum(g_c, axis=2)
        b_last = b[:, :, -1:, :]
        inter = jnp.einsum('bhck,bhkv->bhcv', q_c * jnp.exp(b), S)
        diff = b[:, :, :, None, :] - b[:, :, None, :, :]
        decay = jnp.exp(jnp.where(causal[:, :, None], diff, -jnp.inf))
        att = jnp.einsum('bhik,bhjk,bhijk->bhij', q_c, k_c, decay)
        intra = jnp.einsum('bhij,bhjv->bhiv', att, v_c)
        S_new = (jnp.exp(b_last[:, :, 0, :])[..., None] * S
                 + jnp.einsum('bhck,bhcv->bhkv', k_c * jnp.exp(b_last - b), v_c))
        return S_new, inter + intra

    S0 = jnp.zeros((B, GLA_HEADS, GLA_HK, GLA_HV), jnp.float32)
    _, o = lax.scan(step, S0, (qc, kc, vc, gc))
    o = o.transpose(1, 0, 3, 2, 4).reshape(B, L, GLA_HEADS, GLA_HV).astype(x.dtype)
    o = rms_norm(o, norm_g).reshape(B, L, GLA_DV)
    o = o * jax.nn.silu(r)
    return o @ w_out


def shared_kv(x, valid, kv_w, kv_bf):
    B, L, _ = x.shape
    k, v, f_logit = jnp.split(x @ kv_w, [D_MODEL, 2 * D_MODEL], axis=-1)
    log_f = jax.nn.log_sigmoid((f_logit + kv_bf).astype(jnp.float32))
    log_f = jnp.where(valid[None, :, None], log_f, 0.0)
    c = jnp.cumsum(log_f, axis=1).transpose(0, 2, 1)
    kh = k.reshape(B, L, FOX_HEADS, FOX_HD).transpose(0, 2, 1, 3)
    vh = v.reshape(B, L, FOX_HEADS, FOX_HD).transpose(0, 2, 1, 3)
    return kh, vh, c


def fox_mixer(x, valid, kh, vh, c, w_in, w_out):
    B, L, _ = x.shape
    n_blocks = L // FOX_BLOCK
    q, og = jnp.split(x @ w_in, [D_MODEL], axis=-1)
    q = q.reshape(B, L, FOX_HEADS, FOX_HD).transpose(0, 2, 1, 3) * (FOX_HD ** -0.5)
    qb = q.reshape(B, FOX_HEADS, n_blocks, FOX_BLOCK, FOX_HD).transpose(2, 0, 1, 3, 4)
    cb = c.reshape(B, FOX_HEADS, n_blocks, FOX_BLOCK).transpose(2, 0, 1, 3)
    pos = jnp.arange(L)

    def block(args):
        i, q_i, c_i = args
        t = i * FOX_BLOCK + jnp.arange(FOX_BLOCK)
        logits = (jnp.einsum('bhqd,bhkd->bhqk', q_i, kh).astype(jnp.float32)
                  + c_i[..., None] - c[:, :, None, :])
        mask = (pos[None, :] <= t[:, None]) & (valid[None, :] | (pos[None, :] == t[:, None]))
        p = jax.nn.softmax(jnp.where(mask, logits, -jnp.inf), axis=-1).astype(vh.dtype)
        return jnp.einsum('bhqk,bhkd->bhqd', p, vh)

    o = lax.map(block, (jnp.arange(n_blocks), qb, cb))
    o = o.transpose(1, 0, 3, 2, 4).reshape(B, L, D_MODEL)
    o = o * jax.nn.sigmoid(og)
    return o @ w_out


def conv_ffn(x, valid, w_up, conv_w, conv_b, w_down):
    h = x @ w_up
    h = jnp.where(valid[None, :, None], h, 0)
    h = lax.conv_general_dilated(
        h, conv_w[:, None, :], window_strides=(1,), padding=[(CONV_W - 1, 0)],
        dimension_numbers=('NWC', 'WIO', 'NWC'), feature_group_count=2 * D_FF) + conv_b
    u, g = jnp.split(h, 2, axis=-1)
    return (jax.nn.silu(g) * u) @ w_down


def setup_inputs(seed: int = 0) -> dict:
    key = jax.random.key(seed)
    ks = jax.random.split(key, 20)

    def nrm(k, shape, fan_in, scale=1.0):
        return jax.random.normal(k, shape, jnp.float32) * (fan_in ** -0.5) * scale

    return {
        "x": jax.random.normal(ks[0], (BATCH, SEQ, D_MODEL), jnp.float32),
        "meta": jax.random.normal(ks[1], (N_META, D_MODEL), jnp.float32),
        "ln_g": 1.0 + 0.02 * jax.random.normal(ks[2], (DEPTH, 2, D_MODEL), jnp.float32),
        "ln_b": 0.02 * jax.random.normal(ks[3], (DEPTH, 2, D_MODEL), jnp.float32),
        "gla_w_in": nrm(ks[4], (N_A_LAYERS, D_MODEL, GLA_IN), D_MODEL),
        "gla_w_g2": nrm(ks[5], (N_A_LAYERS, GLA_RANK, GLA_DK), GLA_RANK),
        "gla_b_g2": 0.1 * jax.random.normal(ks[6], (N_A_LAYERS, GLA_DK), jnp.float32),
        "gla_norm_g": 1.0 + 0.02 * jax.random.normal(ks[7], (N_A_LAYERS, GLA_HV), jnp.float32),
        "gla_w_out": nrm(ks[8], (N_A_LAYERS, GLA_DV, D_MODEL), GLA_DV, BETA),
        "kv_w": nrm(ks[9], (D_MODEL, KV_OUT), D_MODEL),
        "kv_bf": 2.0 + 0.5 * jax.random.normal(ks[10], (FOX_HEADS,), jnp.float32),
        "fox_w_in": nrm(ks[11], (N_B_LAYERS, D_MODEL, 2 * D_MODEL), D_MODEL),
        "fox_w_out": nrm(ks[12], (N_B_LAYERS, D_MODEL, D_MODEL), D_MODEL, BETA),
        "ffn_w_up": nrm(ks[13], (DEPTH, D_MODEL, 2 * D_FF), D_MODEL),
        "ffn_conv_w": nrm(ks[14], (DEPTH, CONV_W, 2 * D_FF), CONV_W),
        "ffn_conv_b": 0.02 * jax.random.normal(ks[15], (DEPTH, 2 * D_FF), jnp.float32),
        "ffn_w_down": nrm(ks[16], (DEPTH, D_FF, D_MODEL), D_FF, BETA),
    }


def reference(x, meta, ln_g, ln_b, gla_w_in, gla_w_g2, gla_b_g2, gla_norm_g, gla_w_out,
              kv_w, kv_bf, fox_w_in, fox_w_out, ffn_w_up, ffn_conv_w, ffn_conv_b, ffn_w_down):
    B = x.shape[0]
    L = FRONT + x.shape[1]
    pad = jnp.zeros((B, N_PAD, D_MODEL), x.dtype)
    meta_b = jnp.broadcast_to(meta[None].astype(x.dtype), (B, N_META, D_MODEL))
    h = jnp.concatenate([pad, meta_b, x], axis=1)
    valid = jnp.arange(L) >= N_PAD

    kh = vh = c = None
    for l in range(DEPTH):
        if l < N_A_LAYERS:
            mix = gla_mixer(h, valid, gla_w_in[l], gla_w_g2[l], gla_b_g2[l], gla_norm_g[l], gla_w_out[l])
        else:
            if l == N_A_LAYERS:
                kh, vh, c = shared_kv(h, valid, kv_w, kv_bf)
            j = l - N_A_LAYERS
            mix = fox_mixer(h, valid, kh, vh, c, fox_w_in[j], fox_w_out[j])
        h = layer_norm(ALPHA * h + mix, ln_g[l, 0], ln_b[l, 0])
        ffn = conv_ffn(h, valid, ffn_w_up[l], ffn_conv_w[l], ffn_conv_b[l], ffn_w_down[l])
        h = layer_norm(ALPHA * h + ffn, ln_g[l, 1], ln_b[l, 1])
    return h[:, FRONT:]
```

```python
import functools

import jax
import jax.numpy as jnp
from jax import lax
from jax.experimental import pallas as pl
from jax.experimental.pallas import tpu as pltpu

F32 = jnp.float32
BF16 = jnp.bfloat16

D_MODEL = 2048
SEQ = 8192
DEPTH = 4
N_META = 16
N_A_LAYERS = DEPTH // 2
ALPHA = (2.0 * DEPTH) ** 0.25
LN_EPS = 1e-5

GLA_HEADS = 4
GLA_DK = D_MODEL // 2
GLA_DV = D_MODEL
GLA_HK = GLA_DK // GLA_HEADS
GLA_HV = GLA_DV // GLA_HEADS
GLA_RANK = 16
GLA_TAU = 16.0

FOX_HEADS = 16
FOX_HD = D_MODEL // FOX_HEADS

D_FF = 5632

LANE = 128
L_VALID = N_META + SEQ
ROW_TILE = 640
L_PAD = 13 * ROW_TILE
N_ROW_TILES = L_PAD // ROW_TILE

GLA_CHUNK = 64
GLA_SUB = 16
FF_TILE = 512
N_FF_TILES = D_FF // FF_TILE
FOX_GROUP = 2
CONV_HALO = 8

VMEM_LIMIT = 56 * 1024 * 1024
NEG_BIG = -1e30


def _cparams(*sem):
    return pltpu.CompilerParams(dimension_semantics=sem, vmem_limit_bytes=VMEM_LIMIT)


def _layer_norm(y, g, b):
    mu = jnp.mean(y, axis=-1, keepdims=True)
    d = y - mu
    var = jnp.mean(d * d, axis=-1, keepdims=True)
    return d * lax.rsqrt(var + LN_EPS) * g + b


def _log_sigmoid(z):
    return jnp.minimum(z, 0.0) - jnp.log1p(jnp.exp(-jnp.abs(z)))


def _split3(x):
    hi = x.astype(BF16)
    r1 = x - hi.astype(F32)
    mid = r1.astype(BF16)
    lo = (r1 - mid.astype(F32)).astype(BF16)
    return hi, mid, lo


def _mm_kernel(x_ref, w_ref, s_ref, o_ref):
    acc = jnp.dot(x_ref[...], w_ref[...], preferred_element_type=F32)
    o_ref[...] = (acc * s_ref[...]).astype(o_ref.dtype)


def _matmul(x, w, col_scale, out_dtype, tn):
    k = x.shape[1]
    n = w.shape[1]
    return pl.pallas_call(
        _mm_kernel,
        grid=(N_ROW_TILES, n // tn),
        in_specs=[
            pl.BlockSpec((ROW_TILE, k), lambda i, j: (i, 0)),
            pl.BlockSpec((k, tn), lambda i, j: (0, j)),
            pl.BlockSpec((1, tn), lambda i, j: (0, j)),
        ],
        out_specs=pl.BlockSpec((ROW_TILE, tn), lambda i, j: (i, j)),
        out_shape=jax.ShapeDtypeStruct((L_PAD, n), out_dtype),
        compiler_params=_cparams("parallel", "parallel"),
        name="proj_matmul",
    )(x, w, col_scale)


def _proj_ln_kernel(x_ref, w_ref, h_ref, g_ref, b_ref, of_ref, ob_ref):
    acc = jnp.dot(x_ref[...], w_ref[...], preferred_element_type=F32)
    out = _layer_norm(ALPHA * h_ref[...] + acc, g_ref[...], b_ref[...])
    of_ref[...] = out
    ob_ref[...] = out.astype(BF16)


def _proj_residual_ln(x, w, h, g, b):
    row = lambda i: (i, 0)
    fixed = lambda i: (0, 0)
    return pl.pallas_call(
        _proj_ln_kernel,
        grid=(N_ROW_TILES,),
        in_specs=[
            pl.BlockSpec((ROW_TILE, D_MODEL), row),
            pl.BlockSpec((D_MODEL, D_MODEL), fixed),
            pl.BlockSpec((ROW_TILE, D_MODEL), row),
            pl.BlockSpec((1, D_MODEL), fixed),
            pl.BlockSpec((1, D_MODEL), fixed),
        ],
        out_specs=[pl.BlockSpec((ROW_TILE, D_MODEL), row), pl.BlockSpec((ROW_TILE, D_MODEL), row)],
        out_shape=[jax.ShapeDtypeStruct((L_PAD, D_MODEL), F32),
                   jax.ShapeDtypeStruct((L_PAD, D_MODEL), BF16)],
        compiler_params=_cparams("parallel"),
        name="out_proj_ln",
    )(x, w, h, g, b)


def _ffn_kernel(xb_ref, xf_ref, wu_ref, wg_ref, cwu_ref, cwg_ref, cbu_ref, cbg_ref, wd_ref,
                g_ref, b_ref, of_ref, ob_ref, ext_ref, carry_ref):
    i = pl.program_id(0)
    j = pl.program_id(1)
    xb = xb_ref[...]

    def conv(slot, w_ref, cw_ref, cb_ref):
        hval = jnp.dot(xb, w_ref[...], preferred_element_type=F32)

        @pl.when(i == 0)
        def _():
            ext_ref[slot, 0:CONV_HALO, :] = jnp.zeros((CONV_HALO, FF_TILE), F32)

        @pl.when(i > 0)
        def _():
            ext_ref[slot, 0:CONV_HALO, :] = carry_ref[j, slot]

        ext_ref[slot, CONV_HALO:CONV_HALO + ROW_TILE, :] = hval
        carry_ref[j, slot] = hval[ROW_TILE - CONV_HALO:, :]
        cw = cw_ref[...]
        return (cw[2:3, :] * hval
                + cw[1:2, :] * ext_ref[slot, CONV_HALO - 1:CONV_HALO - 1 + ROW_TILE, :]
                + cw[0:1, :] * ext_ref[slot, CONV_HALO - 2:CONV_HALO - 2 + ROW_TILE, :]
                + cb_ref[...])

    u = conv(0, wu_ref, cwu_ref, cbu_ref)
    gt = conv(1, wg_ref, cwg_ref, cbg_ref)
    act = (gt * jax.nn.sigmoid(gt) * u).astype(BF16)
    contrib = jnp.dot(act, wd_ref[...], preferred_element_type=F32)

    @pl.when(j == 0)
    def _():
        of_ref[...] = contrib

    @pl.when(j > 0)
    def _():
        of_ref[...] += contrib

    @pl.when(j == N_FF_TILES - 1)
    def _():
        out = _layer_norm(ALPHA * xf_ref[...] + of_ref[...], g_ref[...], b_ref[...])
        of_ref[...] = out
        ob_ref[...] = out.astype(BF16)


def _conv_ffn_ln(hb, hf, w_up, conv_w, conv_b, w_down, g, b):
    row = lambda i, j: (i, 0)
    fixed = lambda i, j: (0, 0)
    ucol = lambda i, j: (0, j)
    gcol = lambda i, j: (0, N_FF_TILES + j)
    return pl.pallas_call(
        _ffn_kernel,
        grid=(N_ROW_TILES, N_FF_TILES),
        in_specs=[
            pl.BlockSpec((ROW_TILE, D_MODEL), row, pipeline_mode=pl.Buffered(1)),
            pl.BlockSpec((ROW_TILE, D_MODEL), row, pipeline_mode=pl.Buffered(1)),
            pl.BlockSpec((D_MODEL, FF_TILE), ucol),
            pl.BlockSpec((D_MODEL, FF_TILE), gcol),
            pl.BlockSpec((3, FF_TILE), ucol),
            pl.BlockSpec((3, FF_TILE), gcol),
            pl.BlockSpec((1, FF_TILE), ucol),
            pl.BlockSpec((1, FF_TILE), gcol),
            pl.BlockSpec((FF_TILE, D_MODEL), lambda i, j: (j, 0)),
            pl.BlockSpec((1, D_MODEL), fixed),
            pl.BlockSpec((1, D_MODEL), fixed),
        ],
        out_specs=[pl.BlockSpec((ROW_TILE, D_MODEL), row), pl.BlockSpec((ROW_TILE, D_MODEL), row)],
        out_shape=[jax.ShapeDtypeStruct((L_PAD, D_MODEL), F32),
                   jax.ShapeDtypeStruct((L_PAD, D_MODEL), BF16)],
        scratch_shapes=[
            pltpu.VMEM((2, ROW_TILE + CONV_HALO, FF_TILE), F32),
            pltpu.VMEM((N_FF_TILES, 2, CONV_HALO, FF_TILE), F32),
        ],
        compiler_params=_cparams("arbitrary", "arbitrary"),
        name="conv_ffn_ln",
    )(hb, hf, w_up, w_up, conv_w, conv_w, conv_b, conv_b, w_down, g, b)


def _gla_kernel(q_ref, k_ref, v_ref, r_ref, gl_ref, w2_ref, b2_ref, ng_ref, o_ref,
                st_ref, b_sc, q_sc, k_sc):
    c, sub = GLA_CHUNK, GLA_SUB

    @pl.when(pl.program_id(1) == 0)
    def _():
        st_ref[...] = jnp.zeros_like(st_ref)

    z = jnp.dot(gl_ref[...].astype(BF16), w2_ref[...], preferred_element_type=F32) + b2_ref[...]
    log_a = _log_sigmoid(z) * (1.0 / GLA_TAU)

    rid = lax.broadcasted_iota(jnp.int32, (c, c), 0)
    cid = lax.broadcasted_iota(jnp.int32, (c, c), 1)
    tri = jnp.where(rid >= cid, 1.0, 0.0).astype(BF16)
    b = sum(jnp.dot(tri, t, preferred_element_type=F32) for t in _split3(log_a))
    b_last = b[c - 1:c, :]

    qf = q_ref[...].astype(F32)
    kf = k_ref[...].astype(F32)
    v = v_ref[...]
    b_sc[...] = b
    q_sc[...] = qf
    k_sc[...] = kf

    st = st_ref[...]
    qd = (qf * jnp.exp(b)).astype(BF16)
    o = lax.dot_general(qd, st.astype(BF16), (((1,), (1,)), ((), ())), preferred_element_type=F32)

    sub_rid = lax.broadcasted_iota(jnp.int32, (sub, 1), 0)
    sub_cid = lax.broadcasted_iota(jnp.int32, (sub, c), 1)
    att_rows = []
    for blk in range(c // sub):
        r0 = blk * sub
        b_blk = b_sc[r0:r0 + sub, :]
        q_blk = q_sc[r0:r0 + sub, :]
        att = jnp.zeros((sub, c), F32)
        for jj in range(sub):
            col = r0 + jj
            e = jnp.exp(jnp.where(sub_rid >= jj, b_blk - b_sc[col:col + 1, :], NEG_BIG))
            s = jnp.sum(q_blk * k_sc[col:col + 1, :] * e, axis=-1, keepdims=True)
            att = jnp.where(sub_cid == col, s, att)
        if blk > 0:
            b_ref0 = b_sc[r0:r0 + 1, :]
            q_off = (q_blk * jnp.exp(b_blk - b_ref0)).astype(BF16)
            k_off = (kf * jnp.exp(jnp.minimum(b_ref0 - b, 0.0))).astype(BF16)
            a_off = lax.dot_general(q_off, k_off, (((1,), (1,)), ((), ())),
                                    preferred_element_type=F32)
            att = jnp.where(sub_cid < r0, a_off, att)
        att_rows.append(att)
    att_all = jnp.concatenate(att_rows, axis=0).astype(BF16)
    o = o + jnp.dot(att_all, v, preferred_element_type=F32)

    kd = (kf * jnp.exp(b_last - b)).astype(BF16)
    st_ref[...] = st * jnp.exp(b_last) + lax.dot_general(
        v, kd, (((0,), (0,)), ((), ())), preferred_element_type=F32)

    y = o * lax.rsqrt(jnp.mean(o * o, axis=-1, keepdims=True) + LN_EPS) * ng_ref[...]
    rg = r_ref[...].astype(F32)
    o_ref[...] = (y * (rg * jax.nn.sigmoid(rg))).astype(BF16)


def _gla_core(qkvr, g_low, w_g2, b_g2, norm_g):
    c = GLA_CHUNK
    kq = GLA_DK // GLA_HK
    kv_ = 2 * GLA_DK // GLA_HV
    return pl.pallas_call(
        _gla_kernel,
        grid=(GLA_HEADS, L_PAD // c),
        in_specs=[
            pl.BlockSpec((c, GLA_HK), lambda h, n: (n, h)),
            pl.BlockSpec((c, GLA_HK), lambda h, n: (n, kq + h)),
            pl.BlockSpec((c, GLA_HV), lambda h, n: (n, kv_ + h)),
            pl.BlockSpec((c, GLA_HV), lambda h, n: (n, kv_ + GLA_HEADS + h)),
            pl.BlockSpec((c, LANE), lambda h, n: (n, 0)),
            pl.BlockSpec((LANE, GLA_HK), lambda h, n: (0, h)),
            pl.BlockSpec((1, GLA_HK), lambda h, n: (0, h)),
            pl.BlockSpec((1, GLA_HV), lambda h, n: (0, 0)),
        ],
        out_specs=pl.BlockSpec((c, GLA_HV), lambda h, n: (n, h)),
        out_shape=jax.ShapeDtypeStruct((L_PAD, GLA_DV), BF16),
        scratch_shapes=[
            pltpu.VMEM((GLA_HV, GLA_HK), F32),
            pltpu.VMEM((c, GLA_HK), F32),
            pltpu.VMEM((c, GLA_HK), F32),
            pltpu.VMEM((c, GLA_HK), F32),
        ],
        compiler_params=_cparams("arbitrary", "arbitrary"),
        name="gla_core",
    )(qkvr, qkvr, qkvr, qkvr, g_low, w_g2, b_g2, norm_g)


def _cum_gate_kernel(fl_ref, bf_ref, o_ref, carry_ref):
    t = ROW_TILE

    @pl.when(pl.program_id(0) == 0)
    def _():
        carry_ref[...] = jnp.zeros_like(carry_ref)

    log_f = _log_sigmoid(fl_ref[...] + bf_ref[...])
    rid = lax.broadcasted_iota(jnp.int32, (t, t), 0)
    cid = lax.broadcasted_iota(jnp.int32, (t, t), 1)
    upper = jnp.where(rid <= cid, 1.0, 0.0).astype(BF16)
    cum = sum(lax.dot_general(p, upper, (((0,), (0,)), ((), ())), preferred_element_type=F32)
              for p in _split3(log_f))
    cum = cum + carry_ref[...]
    o_ref[...] = cum
    carry_ref[...] = cum[:, t - 1:t]


def _cum_gate(f_logit, bias):
    return pl.pallas_call(
        _cum_gate_kernel,
        grid=(N_ROW_TILES,),
        in_specs=[pl.BlockSpec((ROW_TILE, LANE), lambda i: (i, 0)),
                  pl.BlockSpec((1, LANE), lambda i: (0, 0))],
        out_specs=pl.BlockSpec((LANE, ROW_TILE), lambda i: (0, i)),
        out_shape=jax.ShapeDtypeStruct((LANE, L_PAD), F32),
        scratch_shapes=[pltpu.VMEM((LANE, 1), F32)],
        compiler_params=_cparams("arbitrary"),
        name="cum_forget_gate",
    )(f_logit, bias)


def _fox_kernel(q_ref, k_ref, v_ref, og_ref, c_ref, o_ref):
    t = ROW_TILE
    qi = pl.program_id(1)
    rid = lax.broadcasted_iota(jnp.int32, (t, t), 0)
    cid = lax.broadcasted_iota(jnp.int32, (t, t), 1)
    causal = cid <= rid
    for g in range(FOX_GROUP):
        cols = slice(g * FOX_HD, (g + 1) * FOX_HD)
        q = q_ref[:, cols]
        c_q0 = c_ref[g, :, pl.ds(pl.multiple_of(qi * t, LANE), LANE)][:, 0:1]

        def scores(kb):
            start = pl.multiple_of(kb * t, LANE)
            s = lax.dot_general(q, k_ref[pl.ds(start, t), cols], (((1,), (1,)), ((), ())),
                                preferred_element_type=F32)
            return s + (c_q0 - c_ref[g, :, pl.ds(start, t)]), start

        def update(carry, s, start):
            m, l, acc = carry
            m_new = jnp.maximum(m, jnp.max(s, axis=-1, keepdims=True))
            p = jnp.exp(s - m_new)
            alpha = jnp.exp(m - m_new)
            l_new = alpha * l + jnp.sum(p, axis=-1, keepdims=True)
            pv = jnp.dot(p.astype(BF16), v_ref[pl.ds(start, t), cols], preferred_element_type=F32)
            return m_new, l_new, alpha * acc + pv

        def body(kb, carry):
            s, start = scores(kb)
            return update(carry, s, start)

        init = (jnp.full((t, 1), NEG_BIG, F32), jnp.zeros((t, 1), F32), jnp.zeros((t, FOX_HD), F32))
        carry = lax.fori_loop(0, qi, body, init)
        s, start = scores(qi)
        _, l, acc = update(carry, jnp.where(causal, s, NEG_BIG), start)
        og = og_ref[:, cols].astype(F32)
        o_ref[:, cols] = (acc / l * jax.nn.sigmoid(og)).astype(BF16)


def _fox_attention(q_og, kv, c_t):
    w = FOX_GROUP * FOX_HD
    n_groups = FOX_HEADS // FOX_GROUP
    return pl.pallas_call(
        _fox_kernel,
        grid=(n_groups, N_ROW_TILES),
        in_specs=[
            pl.BlockSpec((ROW_TILE, w), lambda hg, qi: (qi, hg)),
            pl.BlockSpec((L_PAD, w), lambda hg, qi: (0, hg)),
            pl.BlockSpec((L_PAD, w), lambda hg, qi: (0, n_groups + hg)),
            pl.BlockSpec((ROW_TILE, w), lambda hg, qi: (qi, n_groups + hg)),
            pl.BlockSpec((FOX_GROUP, 1, L_PAD), lambda hg, qi: (hg, 0, 0)),
        ],
        out_specs=pl.BlockSpec((ROW_TILE, w), lambda hg, qi: (qi, hg)),
        out_shape=jax.ShapeDtypeStruct((L_PAD, D_MODEL), BF16),
        compiler_params=_cparams("parallel", "parallel"),
        name="fox_attention",
    )(q_og, kv, kv, q_og, c_t)


def _pad_cols(w, n):
    return jnp.pad(w, ((0, 0), (0, n - w.shape[1])))


def kernel(x, meta, ln_g, ln_b, gla_w_in, gla_w_g2, gla_b_g2, gla_norm_g, gla_w_out, kv_w, kv_bf,
           fox_w_in, fox_w_out, ffn_w_up, ffn_conv_w, ffn_conv_b, ffn_w_down):
    assert x.shape == (1, SEQ, D_MODEL)
    tail = jnp.zeros((L_PAD - L_VALID, D_MODEL), F32)
    hf = jnp.concatenate([meta.astype(F32), x[0], tail], axis=0)
    hb = hf.astype(BF16)

    ones = lambda n: jnp.ones((1, n), F32)
    row = lambda v: v.reshape(1, -1)
    g_lo, g_hi = 2 * GLA_DK + GLA_DV, 2 * GLA_DK + GLA_DV + GLA_RANK

    c_t = kv = None
    for l in range(DEPTH):
        if l < N_A_LAYERS:
            w_in = gla_w_in[l]
            w_qkvr = jnp.concatenate([w_in[:, :g_lo], w_in[:, g_hi:]], axis=1).astype(BF16)
            w_g1 = _pad_cols(w_in[:, g_lo:g_hi], LANE).astype(BF16)
            q_scale = jnp.concatenate([jnp.full((1, GLA_DK), GLA_HK ** -0.5, F32),
                                       ones(w_qkvr.shape[1] - GLA_DK)], axis=1)
            qkvr = _matmul(hb, w_qkvr, q_scale, BF16, 512)
            g_low = _matmul(hb, w_g1, ones(LANE), F32, LANE)
            w_g2 = jnp.pad(gla_w_g2[l], ((0, LANE - GLA_RANK), (0, 0))).astype(BF16)
            mix = _gla_core(qkvr, g_low, w_g2, row(gla_b_g2[l]), row(gla_norm_g[l]))
            w_out = gla_w_out[l].astype(BF16)
        else:
            if l == N_A_LAYERS:
                kv = _matmul(hb, kv_w[:, :2 * D_MODEL].astype(BF16), ones(2 * D_MODEL), BF16, 512)
                f_logit = _matmul(hb, _pad_cols(kv_w[:, 2 * D_MODEL:], LANE).astype(BF16),
                                  ones(LANE), F32, LANE)
                c_t = _cum_gate(f_logit, _pad_cols(row(kv_bf), LANE))[:FOX_HEADS, None, :]
            j = l - N_A_LAYERS
            q_scale = jnp.concatenate([jnp.full((1, D_MODEL), FOX_HD ** -0.5, F32), ones(D_MODEL)],
                                      axis=1)
            q_og = _matmul(hb, fox_w_in[j].astype(BF16), q_scale, BF16, 512)
            mix = _fox_attention(q_og, kv, c_t)
            w_out = fox_w_out[j].astype(BF16)
        hf, hb = _proj_residual_ln(mix, w_out, hf, row(ln_g[l, 0]), row(ln_b[l, 0]))
        hf, hb = _conv_ffn_ln(hb, hf, ffn_w_up[l].astype(BF16), ffn_conv_w[l], row(ffn_conv_b[l]),
                              ffn_w_down[l].astype(BF16), row(ln_g[l, 1]), row(ln_b[l, 1]))
    return hf[N_META:L_VALID][None]
```

```python
import functools

import jax
import jax.numpy as jnp
from jax import lax
from jax.experimental import pallas as pl
from jax.experimental.pallas import tpu as pltpu

F32 = jnp.float32
BF16 = jnp.bfloat16

D_MODEL = 2048
SEQ = 8192
DEPTH = 4
N_META = 16
N_A_LAYERS = DEPTH // 2
ALPHA = (2.0 * DEPTH) ** 0.25
LN_EPS = 1e-5

GLA_HEADS = 4
GLA_DK = D_MODEL // 2
GLA_DV = D_MODEL
GLA_HK = GLA_DK // GLA_HEADS
GLA_HV = GLA_DV // GLA_HEADS
GLA_RANK = 16
GLA_TAU = 16.0

FOX_HEADS = 16
FOX_HD = D_MODEL // FOX_HEADS

D_FF = 5632

LANE = 128
L_VALID = N_META + SEQ
ROW_TILE = 640
L_PAD = 13 * ROW_TILE
N_ROW_TILES = L_PAD // ROW_TILE

GLA_CHUNK = 64
GLA_SUB = 16
FF_TILE = 512
FF_SUB = 256
N_FF_TILES = D_FF // FF_TILE
FOX_GROUP = 4
CONV_HALO = 8

VMEM_LIMIT = 56 * 1024 * 1024
NEG_BIG = -1e30
LOG2_E = 1.4426950408889634


def _cparams(*sem):
    return pltpu.CompilerParams(dimension_semantics=sem, vmem_limit_bytes=VMEM_LIMIT)


def _layer_norm(y, g, b):
    mu = jnp.mean(y, axis=-1, keepdims=True)
    d = y - mu
    var = jnp.mean(d * d, axis=-1, keepdims=True)
    return d * lax.rsqrt(var + LN_EPS) * g + b


def _log_sigmoid(z):
    return jnp.minimum(z, 0.0) - jnp.log1p(jnp.exp(-jnp.abs(z)))


def _split3(x):
    hi = x.astype(BF16)
    r1 = x - hi.astype(F32)
    mid = r1.astype(BF16)
    lo = (r1 - mid.astype(F32)).astype(BF16)
    return hi, mid, lo


def _mm_kernel(x_ref, w_ref, s_ref, o_ref):
    acc = jnp.dot(x_ref[...], w_ref[...], preferred_element_type=F32)
    o_ref[...] = (acc * s_ref[...]).astype(o_ref.dtype)


def _matmul(x, w, col_scale, out_dtype, tn):
    k = x.shape[1]
    n = w.shape[1]
    return pl.pallas_call(
        _mm_kernel,
        grid=(N_ROW_TILES, n // tn),
        in_specs=[
            pl.BlockSpec((ROW_TILE, k), lambda i, j: (i, 0)),
            pl.BlockSpec((k, tn), lambda i, j: (0, j)),
            pl.BlockSpec((1, tn), lambda i, j: (0, j)),
        ],
        out_specs=pl.BlockSpec((ROW_TILE, tn), lambda i, j: (i, j)),
        out_shape=jax.ShapeDtypeStruct((L_PAD, n), out_dtype),
        compiler_params=_cparams("parallel", "parallel"),
        name="proj_matmul",
    )(x, w, col_scale)


def _proj_ln_kernel(x_ref, w_ref, h_ref, g_ref, b_ref, of_ref, ob_ref):
    acc = jnp.dot(x_ref[...], w_ref[...], preferred_element_type=F32)
    out = _layer_norm(ALPHA * h_ref[...] + acc, g_ref[...], b_ref[...])
    of_ref[...] = out
    ob_ref[...] = out.astype(BF16)


def _proj_residual_ln(x, w, h, g, b):
    row = lambda i: (i, 0)
    fixed = lambda i: (0, 0)
    return pl.pallas_call(
        _proj_ln_kernel,
        grid=(N_ROW_TILES,),
        in_specs=[
            pl.BlockSpec((ROW_TILE, D_MODEL), row),
            pl.BlockSpec((D_MODEL, D_MODEL), fixed),
            pl.BlockSpec((ROW_TILE, D_MODEL), row),
            pl.BlockSpec((1, D_MODEL), fixed),
            pl.BlockSpec((1, D_MODEL), fixed),
        ],
        out_specs=[pl.BlockSpec((ROW_TILE, D_MODEL), row), pl.BlockSpec((ROW_TILE, D_MODEL), row)],
        out_shape=[jax.ShapeDtypeStruct((L_PAD, D_MODEL), F32),
                   jax.ShapeDtypeStruct((L_PAD, D_MODEL), BF16)],
        compiler_params=_cparams("parallel"),
        name="out_proj_ln",
    )(x, w, h, g, b)


def _ffn_kernel(xb_ref, xf_ref, wu_ref, wg_ref, cwu_ref, cwg_ref, cbu_ref, cbg_ref, wd_ref,
                g_ref, b_ref, of_ref, ob_ref, ext_ref, carry_ref):
    i = pl.program_id(0)
    j = pl.program_id(1)

    @pl.when(i == 0)
    def _():
        carry_ref[j] = jnp.zeros(carry_ref.shape[1:], F32)

    @pl.when(j == 0)
    def _():
        of_ref[...] = jnp.zeros_like(of_ref)

    xb = xb_ref[...]

    def conv(slot, cols, w_ref, cw_ref, cb_ref):
        hval = jnp.dot(xb, w_ref[:, cols], preferred_element_type=F32)
        ext_ref[slot, 0:CONV_HALO, :] = carry_ref[j, slot]
        ext_ref[slot, CONV_HALO:CONV_HALO + ROW_TILE, :] = hval
        carry_ref[j, slot] = hval[ROW_TILE - CONV_HALO:, :]
        cw = cw_ref[:, cols]
        return (cw[2:3, :] * hval
                + cw[1:2, :] * ext_ref[slot, CONV_HALO - 1:CONV_HALO - 1 + ROW_TILE, :]
                + cw[0:1, :] * ext_ref[slot, CONV_HALO - 2:CONV_HALO - 2 + ROW_TILE, :]
                + cb_ref[:, cols])

    contrib = None
    for c in range(FF_TILE // FF_SUB):
        cols = slice(c * FF_SUB, (c + 1) * FF_SUB)
        u = conv(2 * c, cols, wu_ref, cwu_ref, cbu_ref)
        gt = conv(2 * c + 1, cols, wg_ref, cwg_ref, cbg_ref)
        act = (gt * jax.nn.sigmoid(gt) * u).astype(BF16)
        part = jnp.dot(act, wd_ref[cols, :], preferred_element_type=F32)
        contrib = part if contrib is None else contrib + part
    of_ref[...] += contrib

    @pl.when(j == N_FF_TILES - 1)
    def _():
        out = _layer_norm(ALPHA * xf_ref[...] + of_ref[...], g_ref[...], b_ref[...])
        of_ref[...] = out
        ob_ref[...] = out.astype(BF16)


def _conv_ffn_ln(hb, hf, w_up, conv_w, conv_b, w_down, g, b):
    row = lambda i, j: (i, 0)
    fixed = lambda i, j: (0, 0)
    ucol = lambda i, j: (0, j)
    gcol = lambda i, j: (0, N_FF_TILES + j)
    return pl.pallas_call(
        _ffn_kernel,
        grid=(N_ROW_TILES, N_FF_TILES),
        in_specs=[
            pl.BlockSpec((ROW_TILE, D_MODEL), row, pipeline_mode=pl.Buffered(1)),
            pl.BlockSpec((ROW_TILE, D_MODEL), row, pipeline_mode=pl.Buffered(1)),
            pl.BlockSpec((D_MODEL, FF_TILE), ucol),
            pl.BlockSpec((D_MODEL, FF_TILE), gcol),
            pl.BlockSpec((3, FF_TILE), ucol),
            pl.BlockSpec((3, FF_TILE), gcol),
            pl.BlockSpec((1, FF_TILE), ucol),
            pl.BlockSpec((1, FF_TILE), gcol),
            pl.BlockSpec((FF_TILE, D_MODEL), lambda i, j: (j, 0)),
            pl.BlockSpec((1, D_MODEL), fixed),
            pl.BlockSpec((1, D_MODEL), fixed),
        ],
        out_specs=[pl.BlockSpec((ROW_TILE, D_MODEL), row), pl.BlockSpec((ROW_TILE, D_MODEL), row)],
        out_shape=[jax.ShapeDtypeStruct((L_PAD, D_MODEL), F32),
                   jax.ShapeDtypeStruct((L_PAD, D_MODEL), BF16)],
        scratch_shapes=[
            pltpu.VMEM((2 * FF_TILE // FF_SUB, ROW_TILE + CONV_HALO, FF_SUB), F32),
            pltpu.VMEM((N_FF_TILES, 2 * FF_TILE // FF_SUB, CONV_HALO, FF_SUB), F32),
        ],
        compiler_params=_cparams("arbitrary", "arbitrary"),
        name="conv_ffn_ln",
    )(hb, hf, w_up, w_up, conv_w, conv_w, conv_b, conv_b, w_down, g, b)


def _gla_kernel(q_ref, k_ref, v_ref, r_ref, gl_ref, w2_ref, b2_ref, ng_ref, o_ref,
                st_ref, b_sc, q_sc, k_sc):
    c, sub = GLA_CHUNK, GLA_SUB

    @pl.when(pl.program_id(1) == 0)
    def _():
        st_ref[...] = jnp.zeros_like(st_ref)

    z = jnp.dot(gl_ref[...].astype(BF16), w2_ref[...], preferred_element_type=F32) + b2_ref[...]
    log_a = _log_sigmoid(z) * (1.0 / GLA_TAU)

    rid = lax.broadcasted_iota(jnp.int32, (c, c), 0)
    cid = lax.broadcasted_iota(jnp.int32, (c, c), 1)
    tri = jnp.where(rid >= cid, 1.0, 0.0).astype(BF16)
    b = sum(jnp.dot(tri, t, preferred_element_type=F32) for t in _split3(log_a))
    b_last = b[c - 1:c, :]

    qf = q_ref[...].astype(F32)
    kf = k_ref[...].astype(F32)
    v = v_ref[...]
    b_sc[...] = b
    q_sc[...] = qf
    k_sc[...] = kf

    st = st_ref[...]
    qd = (qf * jnp.exp(b)).astype(BF16)
    o = lax.dot_general(qd, st.astype(BF16), (((1,), (1,)), ((), ())), preferred_element_type=F32)

    sub_rid = lax.broadcasted_iota(jnp.int32, (sub, 1), 0)
    sub_cid = lax.broadcasted_iota(jnp.int32, (sub, c), 1)
    att_rows = []
    for blk in range(c // sub):
        r0 = blk * sub
        b_blk = b_sc[r0:r0 + sub, :]
        q_blk = q_sc[r0:r0 + sub, :]
        att = jnp.zeros((sub, c), F32)
        for jj in range(sub):
            col = r0 + jj
            e = jnp.exp(jnp.where(sub_rid >= jj, b_blk - b_sc[col:col + 1, :], NEG_BIG))
            s = jnp.sum(q_blk * k_sc[col:col + 1, :] * e, axis=-1, keepdims=True)
            att = jnp.where(sub_cid == col, s, att)
        if blk > 0:
            b_ref0 = b_sc[r0:r0 + 1, :]
            q_off = (q_blk * jnp.exp(b_blk - b_ref0)).astype(BF16)
            k_off = (kf * jnp.exp(jnp.minimum(b_ref0 - b, 0.0))).astype(BF16)
            a_off = lax.dot_general(q_off, k_off, (((1,), (1,)), ((), ())),
                                    preferred_element_type=F32)
            att = jnp.where(sub_cid < r0, a_off, att)
        att_rows.append(att)
    att_all = jnp.concatenate(att_rows, axis=0).astype(BF16)
    o = o + jnp.dot(att_all, v, preferred_element_type=F32)

    kd = (kf * jnp.exp(b_last - b)).astype(BF16)
    st_ref[...] = st * jnp.exp(b_last) + lax.dot_general(
        v, kd, (((0,), (0,)), ((), ())), preferred_element_type=F32)

    y = o * lax.rsqrt(jnp.mean(o * o, axis=-1, keepdims=True) + LN_EPS) * ng_ref[...]
    rg = r_ref[...].astype(F32)
    o_ref[...] = (y * (rg * jax.nn.sigmoid(rg))).astype(BF16)


def _gla_core(qkvr, g_low, w_g2, b_g2, norm_g):
    c = GLA_CHUNK
    kq = GLA_DK // GLA_HK
    kv_ = 2 * GLA_DK // GLA_HV
    return pl.pallas_call(
        _gla_kernel,
        grid=(GLA_HEADS, L_PAD // c),
        in_specs=[
            pl.BlockSpec((c, GLA_HK), lambda h, n: (n, h)),
            pl.BlockSpec((c, GLA_HK), lambda h, n: (n, kq + h)),
            pl.BlockSpec((c, GLA_HV), lambda h, n: (n, kv_ + h)),
            pl.BlockSpec((c, GLA_HV), lambda h, n: (n, kv_ + GLA_HEADS + h)),
            pl.BlockSpec((c, LANE), lambda h, n: (n, 0)),
            pl.BlockSpec((LANE, GLA_HK), lambda h, n: (0, h)),
            pl.BlockSpec((1, GLA_HK), lambda h, n: (0, h)),
            pl.BlockSpec((1, GLA_HV), lambda h, n: (0, 0)),
        ],
        out_specs=pl.BlockSpec((c, GLA_HV), lambda h, n: (n, h)),
        out_shape=jax.ShapeDtypeStruct((L_PAD, GLA_DV), BF16),
        scratch_shapes=[
            pltpu.VMEM((GLA_HV, GLA_HK), F32),
            pltpu.VMEM((c, GLA_HK), F32),
            pltpu.VMEM((c, GLA_HK), F32),
            pltpu.VMEM((c, GLA_HK), F32),
        ],
        compiler_params=_cparams("arbitrary", "arbitrary"),
        name="gla_core",
    )(qkvr, qkvr, qkvr, qkvr, g_low, w_g2, b_g2, norm_g)


def _cum_gate_kernel(fl_ref, bf_ref, o_ref, carry_ref):
    t = ROW_TILE

    @pl.when(pl.program_id(0) == 0)
    def _():
        carry_ref[...] = jnp.zeros_like(carry_ref)

    log_f = _log_sigmoid(fl_ref[...] + bf_ref[...]) * LOG2_E
    rid = lax.broadcasted_iota(jnp.int32, (t, t), 0)
    cid = lax.broadcasted_iota(jnp.int32, (t, t), 1)
    upper = jnp.where(rid <= cid, 1.0, 0.0).astype(BF16)
    cum = sum(lax.dot_general(p, upper, (((0,), (0,)), ((), ())), preferred_element_type=F32)
              for p in _split3(log_f))
    cum = cum + carry_ref[...]
    o_ref[...] = cum
    carry_ref[...] = cum[:, t - 1:t]


def _cum_gate(f_logit, bias):
    return pl.pallas_call(
        _cum_gate_kernel,
        grid=(N_ROW_TILES,),
        in_specs=[pl.BlockSpec((ROW_TILE, LANE), lambda i: (i, 0)),
                  pl.BlockSpec((1, LANE), lambda i: (0, 0))],
        out_specs=pl.BlockSpec((LANE, ROW_TILE), lambda i: (0, i)),
        out_shape=jax.ShapeDtypeStruct((LANE, L_PAD), F32),
        scratch_shapes=[pltpu.VMEM((LANE, 1), F32)],
        compiler_params=_cparams("arbitrary"),
        name="cum_forget_gate",
    )(f_logit, bias)


def _fox_kernel(q_ref, k_ref, v_ref, og_ref, c_ref, o_ref, m_sc, acc_sc):
    t = ROW_TILE
    qi = pl.program_id(1)
    rid = lax.broadcasted_iota(jnp.int32, (t, t), 0)
    cid = lax.broadcasted_iota(jnp.int32, (t, t), 1)
    causal = cid <= rid
    heads = range(FOX_GROUP)
    cols = [slice(g * FOX_HD, (g + 1) * FOX_HD) for g in heads]
    q_start = pl.multiple_of(qi * t, LANE)
    c_q0 = [c_ref[g, :, pl.ds(q_start, LANE)][:, 0:1] for g in heads]

    ones = jnp.ones((t, FOX_HD), BF16)

    def scores(kb):
        start = pl.multiple_of(kb * t, LANE)
        return tuple(
            lax.dot_general(q_ref[:, cols[g]], k_ref[pl.ds(start, t), cols[g]],
                            (((1,), (1,)), ((), ())), preferred_element_type=F32)
            + (c_q0[g] - c_ref[g, :, pl.ds(start, t)]) for g in heads)

    def update(kb, scores_kb, mask):
        start = pl.multiple_of(kb * t, LANE)
        for g in heads:
            s = scores_kb[g]
            if mask:
                s = jnp.where(causal, s, NEG_BIG)
            m = m_sc[g]
            m_new = jnp.maximum(m, jnp.max(s, axis=-1, keepdims=True))
            p = jnp.exp2(s - jnp.concatenate([m_new] * (t // LANE), axis=1)).astype(BF16)
            v_aug = jnp.concatenate([v_ref[pl.ds(start, t), cols[g]], ones], axis=1)
            alpha = jnp.exp2(m - m_new)
            acc_sc[g] = jnp.concatenate([alpha, alpha], axis=1) * acc_sc[g] + jnp.dot(
                p, v_aug, preferred_element_type=F32)
            m_sc[g] = m_new

    for g in heads:
        m_sc[g] = jnp.full((t, LANE), NEG_BIG, F32)
        acc_sc[g] = jnp.zeros((t, 2 * FOX_HD), F32)

    def body(kb, _):
        update(kb, scores(kb), False)
        return 0

    lax.fori_loop(0, qi, body, 0)
    update(qi, scores(qi), True)
    for g in heads:
        acc = acc_sc[g]
        og = og_ref[:, cols[g]].astype(F32)
        o_ref[:, cols[g]] = (acc[:, :FOX_HD] / acc[:, FOX_HD:] * jax.nn.sigmoid(og)).astype(BF16)


def _fox_attention(q_og, kv, c_t):
    w = FOX_GROUP * FOX_HD
    n_groups = FOX_HEADS // FOX_GROUP
    return pl.pallas_call(
        _fox_kernel,
        grid=(n_groups, N_ROW_TILES),
        in_specs=[
            pl.BlockSpec((ROW_TILE, w), lambda hg, qi: (qi, hg)),
            pl.BlockSpec((L_PAD, w), lambda hg, qi: (0, hg), pipeline_mode=pl.Buffered(1)),
            pl.BlockSpec((L_PAD, w), lambda hg, qi: (0, n_groups + hg),
                         pipeline_mode=pl.Buffered(1)),
            pl.BlockSpec((ROW_TILE, w), lambda hg, qi: (qi, n_groups + hg)),
            pl.BlockSpec((FOX_GROUP, 1, L_PAD), lambda hg, qi: (hg, 0, 0)),
        ],
        out_specs=pl.BlockSpec((ROW_TILE, w), lambda hg, qi: (qi, hg)),
        out_shape=jax.ShapeDtypeStruct((L_PAD, D_MODEL), BF16),
        scratch_shapes=[pltpu.VMEM((FOX_GROUP, ROW_TILE, LANE), F32),
                        pltpu.VMEM((FOX_GROUP, ROW_TILE, 2 * FOX_HD), F32)],
        compiler_params=_cparams("parallel", "parallel"),
        name="fox_attention",
    )(q_og, kv, kv, q_og, c_t)


def _pad_cols(w, n):
    return jnp.pad(w, ((0, 0), (0, n - w.shape[1])))


def kernel(x, meta, ln_g, ln_b, gla_w_in, gla_w_g2, gla_b_g2, gla_norm_g, gla_w_out, kv_w, kv_bf,
           fox_w_in, fox_w_out, ffn_w_up, ffn_conv_w, ffn_conv_b, ffn_w_down):
    assert x.shape == (1, SEQ, D_MODEL)
    tail = jnp.zeros((L_PAD - L_VALID, D_MODEL), F32)
    hf = jnp.concatenate([meta.astype(F32), x[0], tail], axis=0)
    hb = hf.astype(BF16)

    ones = lambda n: jnp.ones((1, n), F32)
    row = lambda v: v.reshape(1, -1)
    g_lo, g_hi = 2 * GLA_DK + GLA_DV, 2 * GLA_DK + GLA_DV + GLA_RANK

    c_t = kv = None
    for l in range(DEPTH):
        if l < N_A_LAYERS:
            w_in = gla_w_in[l]
            w_qkvr = jnp.concatenate([w_in[:, :g_lo], w_in[:, g_hi:]], axis=1).astype(BF16)
            w_g1 = _pad_cols(w_in[:, g_lo:g_hi], LANE).astype(BF16)
            q_scale = jnp.concatenate([jnp.full((1, GLA_DK), GLA_HK ** -0.5, F32),
                                       ones(w_qkvr.shape[1] - GLA_DK)], axis=1)
            qkvr = _matmul(hb, w_qkvr, q_scale, BF16, 512)
            g_low = _matmul(hb, w_g1, ones(LANE), F32, LANE)
            w_g2 = jnp.pad(gla_w_g2[l], ((0, LANE - GLA_RANK), (0, 0))).astype(BF16)
            mix = _gla_core(qkvr, g_low, w_g2, row(gla_b_g2[l]), row(gla_norm_g[l]))
            w_out = gla_w_out[l].astype(BF16)
        else:
            if l == N_A_LAYERS:
                kv = _matmul(hb, kv_w[:, :2 * D_MODEL].astype(BF16), ones(2 * D_MODEL), BF16, 512)
                f_logit = _matmul(hb, _pad_cols(kv_w[:, 2 * D_MODEL:], LANE).astype(BF16),
                                  ones(LANE), F32, LANE)
                c_t = _cum_gate(f_logit, _pad_cols(row(kv_bf), LANE))[:FOX_HEADS, None, :]
            j = l - N_A_LAYERS
            q_scale = jnp.concatenate([jnp.full((1, D_MODEL), FOX_HD ** -0.5 * LOG2_E, F32),
                                       ones(D_MODEL)], axis=1)
            q_og = _matmul(hb, fox_w_in[j].astype(BF16), q_scale, BF16, 512)
            mix = _fox_attention(q_og, kv, c_t)
            w_out = fox_w_out[j].astype(BF16)
        hf, hb = _proj_residual_ln(mix, w_out, hf, row(ln_g[l, 0]), row(ln_b[l, 0]))
        hf, hb = _conv_ffn_ln(hb, hf, ffn_w_up[l].astype(BF16), ffn_conv_w[l], row(ffn_conv_b[l]),
                              ffn_w_down[l].astype(BF16), row(ln_g[l, 1]), row(ln_b[l, 1]))
    return hf[N_META:L_VALID][None]
```

```python
import jax
import jax.numpy as jnp
import numpy as np
from jax import lax
from jax.experimental import pallas as pl
from jax.experimental.pallas import tpu as pltpu

F32 = jnp.float32
BF16 = jnp.bfloat16

D_MODEL = 2048
SEQ = 8192
DEPTH = 4
N_META = 16
N_A_LAYERS = DEPTH // 2
ALPHA = (2.0 * DEPTH) ** 0.25
LN_EPS = 1e-5

GLA_HEADS = 4
GLA_DK = D_MODEL // 2
GLA_DV = D_MODEL
GLA_HK = GLA_DK // GLA_HEADS
GLA_HV = GLA_DV // GLA_HEADS
GLA_RANK = 16
GLA_TAU = 16.0

FOX_HEADS = 16
FOX_HD = D_MODEL // FOX_HEADS

D_FF = 5632

LANE = 128
L_VALID = N_META + SEQ
ROW_TILE = 640
L_PAD = 13 * ROW_TILE
N_ROW_TILES = L_PAD // ROW_TILE

GLA_CHUNK = 64
PROJ_TILE = 1024
FF_TILE = 512
FF_SUB = 256
N_FF_TILES = D_FF // FF_TILE
FOX_GROUP = 4
CONV_HALO = 8

VMEM_LIMIT = 56 * 1024 * 1024
NEG_BIG = -1e30
LOG2_E = 1.4426950408889634


def _cparams(*sem):
    return pltpu.CompilerParams(dimension_semantics=sem, vmem_limit_bytes=VMEM_LIMIT)


def _layer_norm(y, g, b):
    mu = jnp.mean(y, axis=-1, keepdims=True)
    d = y - mu
    var = jnp.mean(d * d, axis=-1, keepdims=True)
    return d * lax.rsqrt(var + LN_EPS) * g + b


def _log_sigmoid(z):
    return jnp.minimum(z, 0.0) - jnp.log1p(jnp.exp(-jnp.abs(z)))


def _split3(x):
    hi = x.astype(BF16)
    r1 = x - hi.astype(F32)
    mid = r1.astype(BF16)
    lo = (r1 - mid.astype(F32)).astype(BF16)
    return hi, mid, lo


def _mm_kernel(x_ref, w_ref, s_ref, o_ref):
    acc = jnp.dot(x_ref[...], w_ref[...], preferred_element_type=F32)
    o_ref[...] = (acc * s_ref[...]).astype(o_ref.dtype)


def _matmul(x, w, layer, n, col_scale, out_dtype, tn):
    k = x.shape[1]
    return pl.pallas_call(
        _mm_kernel,
        grid=(N_ROW_TILES, n // tn),
        in_specs=[
            pl.BlockSpec((ROW_TILE, k), lambda i, j: (i, 0)),
            pl.BlockSpec((None, k, tn), lambda i, j: (layer, 0, j)),
            pl.BlockSpec((1, tn), lambda i, j: (0, j)),
        ],
        out_specs=pl.BlockSpec((ROW_TILE, tn), lambda i, j: (i, j)),
        out_shape=jax.ShapeDtypeStruct((L_PAD, n), out_dtype),
        compiler_params=_cparams("parallel", "parallel"),
        name="proj_matmul",
    )(x, w, col_scale)


def _proj_ln_kernel(x_ref, w_ref, h_ref, g_ref, b_ref, of_ref, ob_ref):
    half = ROW_TILE // 2
    for rows in (slice(0, half), slice(half, ROW_TILE)):
        acc = jnp.dot(x_ref[rows, :], w_ref[...], preferred_element_type=F32)
        out = _layer_norm(ALPHA * h_ref[rows, :] + acc, g_ref[...], b_ref[...])
        of_ref[rows, :] = out
        ob_ref[rows, :] = out.astype(BF16)


def _proj_residual_ln(x, w, layer, h, g, b):
    row = lambda i: (i, 0)
    fixed = lambda i: (0, 0)
    return pl.pallas_call(
        _proj_ln_kernel,
        grid=(N_ROW_TILES,),
        in_specs=[
            pl.BlockSpec((ROW_TILE, D_MODEL), row),
            pl.BlockSpec((None, D_MODEL, D_MODEL), lambda i: (layer, 0, 0)),
            pl.BlockSpec((ROW_TILE, D_MODEL), row),
            pl.BlockSpec((1, D_MODEL), fixed),
            pl.BlockSpec((1, D_MODEL), fixed),
        ],
        out_specs=[pl.BlockSpec((ROW_TILE, D_MODEL), row), pl.BlockSpec((ROW_TILE, D_MODEL), row)],
        out_shape=[jax.ShapeDtypeStruct((L_PAD, D_MODEL), F32),
                   jax.ShapeDtypeStruct((L_PAD, D_MODEL), BF16)],
        compiler_params=_cparams("parallel"),
        name="out_proj_ln",
    )(x, w, h, g, b)


def _ffn_kernel(xb_ref, xf_ref, wu_ref, wg_ref, cwu_ref, cwg_ref, cbu_ref, cbg_ref, wd_ref,
                g_ref, b_ref, of_ref, ob_ref, ext_ref, carry_ref):
    i = pl.program_id(0)
    j = pl.program_id(1)

    @pl.when(i == 0)
    def _():
        carry_ref[j] = jnp.zeros(carry_ref.shape[1:], F32)

    @pl.when(j == 0)
    def _():
        of_ref[...] = jnp.zeros_like(of_ref)

    xb = xb_ref[...]

    def up(slot, cols, w_ref):
        hval = jnp.dot(xb, w_ref[:, cols], preferred_element_type=F32)
        ext_ref[slot, 0:CONV_HALO, :] = carry_ref[j, slot]
        ext_ref[slot, CONV_HALO:CONV_HALO + ROW_TILE, :] = hval
        carry_ref[j, slot] = hval[ROW_TILE - CONV_HALO:, :]
        return hval

    def conv(slot, cols, hval, cw_ref, cb_ref):
        cw = cw_ref[:, cols]
        return (cw[2:3, :] * hval
                + cw[1:2, :] * ext_ref[slot, CONV_HALO - 1:CONV_HALO - 1 + ROW_TILE, :]
                + cw[0:1, :] * ext_ref[slot, CONV_HALO - 2:CONV_HALO - 2 + ROW_TILE, :]
                + cb_ref[:, cols])

    chunks = [slice(c * FF_SUB, (c + 1) * FF_SUB) for c in range(FF_TILE // FF_SUB)]
    hu = [up(2 * c, cols, wu_ref) for c, cols in enumerate(chunks)]
    hg = [up(2 * c + 1, cols, wg_ref) for c, cols in enumerate(chunks)]
    contrib = None
    for c, cols in enumerate(chunks):
        u = conv(2 * c, cols, hu[c], cwu_ref, cbu_ref)
        gt = conv(2 * c + 1, cols, hg[c], cwg_ref, cbg_ref)
        act = (gt * jax.nn.sigmoid(gt) * u).astype(BF16)
        part = jnp.dot(act, wd_ref[cols, :], preferred_element_type=F32)
        contrib = part if contrib is None else contrib + part
    of_ref[...] += contrib

    @pl.when(j == N_FF_TILES - 1)
    def _():
        out = _layer_norm(ALPHA * xf_ref[...] + of_ref[...], g_ref[...], b_ref[...])
        of_ref[...] = out
        ob_ref[...] = out.astype(BF16)


def _conv_ffn_ln(hb, hf, w_up, conv_w, conv_b, w_down, layer, g, b):
    row = lambda i, j: (i, 0)
    fixed = lambda i, j: (0, 0)
    ucol = lambda i, j: (layer, 0, j)
    gcol = lambda i, j: (layer, 0, N_FF_TILES + j)
    n_slots = 2 * FF_TILE // FF_SUB
    return pl.pallas_call(
        _ffn_kernel,
        grid=(N_ROW_TILES, N_FF_TILES),
        in_specs=[
            pl.BlockSpec((ROW_TILE, D_MODEL), row, pipeline_mode=pl.Buffered(1)),
            pl.BlockSpec((ROW_TILE, D_MODEL), row, pipeline_mode=pl.Buffered(1)),
            pl.BlockSpec((None, D_MODEL, FF_TILE), ucol),
            pl.BlockSpec((None, D_MODEL, FF_TILE), gcol),
            pl.BlockSpec((None, 3, FF_TILE), ucol),
            pl.BlockSpec((None, 3, FF_TILE), gcol),
            pl.BlockSpec((None, 1, FF_TILE), ucol),
            pl.BlockSpec((None, 1, FF_TILE), gcol),
            pl.BlockSpec((None, FF_TILE, D_MODEL), lambda i, j: (layer, j, 0)),
            pl.BlockSpec((1, D_MODEL), fixed),
            pl.BlockSpec((1, D_MODEL), fixed),
        ],
        out_specs=[pl.BlockSpec((ROW_TILE, D_MODEL), row), pl.BlockSpec((ROW_TILE, D_MODEL), row)],
        out_shape=[jax.ShapeDtypeStruct((L_PAD, D_MODEL), F32),
                   jax.ShapeDtypeStruct((L_PAD, D_MODEL), BF16)],
        scratch_shapes=[
            pltpu.VMEM((n_slots, ROW_TILE + CONV_HALO, FF_SUB), F32),
            pltpu.VMEM((N_FF_TILES, n_slots, CONV_HALO, FF_SUB), F32),
        ],
        compiler_params=_cparams("arbitrary", "arbitrary"),
        name="conv_ffn_ln",
    )(hb, hf, w_up, w_up, conv_w, conv_w, conv_b, conv_b, w_down, g, b)


def _gla_tables():
    c = GLA_CHUNK
    i = np.arange(c)[:, None]
    t = np.arange(c)[None, :]
    sums = [t <= i, t > i]
    masks = [i == t]
    s = c // 2
    while s >= 1:
        blk = i // s
        boundary = blk * s
        sums.append((blk % 2 == 1) & (t > boundary) & (t <= i))
        sums.append((blk % 2 == 0) & (t > i) & (t <= boundary + s))
        masks.append((blk % 2 == 1) & (t // s == blk - 1))
        s //= 2
    sums = np.concatenate(sums, axis=0).astype(np.float32)
    masks = np.concatenate(masks, axis=0).astype(np.float32)
    return np.concatenate([sums, sums, sums], axis=1), masks


def _gla_kernel(q_ref, k_ref, v_ref, r_ref, gl_ref, w2_ref, b2_ref, ng_ref, sum_ref, mask_ref,
                o_ref, st_ref):
    c = GLA_CHUNK
    n_levels = mask_ref.shape[0] // c - 1
    nt = (((1,), (1,)), ((), ()))

    @pl.when(pl.program_id(0) == 0)
    def _():
        st_ref[...] = jnp.zeros_like(st_ref)

    z = jnp.dot(gl_ref[...].astype(BF16), w2_ref[...], preferred_element_type=F32) + b2_ref[...]
    log2_a = _log_sigmoid(z) * (LOG2_E / GLA_TAU)
    sum_table = sum_ref[...]
    heads = range(GLA_HEADS)
    kc = [slice(h * GLA_HK, (h + 1) * GLA_HK) for h in heads]
    vc = [slice(h * GLA_HV, (h + 1) * GLA_HV) for h in heads]
    decay = [jnp.exp2(jnp.dot(sum_table, jnp.concatenate(_split3(log2_a[:, kc[h]]), axis=0),
                              preferred_element_type=F32)) for h in heads]
    qf = [q_ref[:, kc[h]].astype(F32) for h in heads]
    kf = [k_ref[:, kc[h]].astype(F32) for h in heads]
    o = [lax.dot_general((qf[h] * decay[h][0:c]).astype(BF16), st_ref[h].astype(BF16), nt,
                         preferred_element_type=F32) for h in heads]
    att = [mask_ref[0:c, :] * lax.dot_general(q_ref[:, kc[h]], k_ref[:, kc[h]], nt,
                                              preferred_element_type=F32) for h in heads]
    for lv in range(n_levels):
        r0 = (2 + 2 * lv) * c
        for h in heads:
            q_lv = (qf[h] * decay[h][r0:r0 + c]).astype(BF16)
            k_lv = (kf[h] * decay[h][r0 + c:r0 + 2 * c]).astype(BF16)
            att[h] = att[h] + mask_ref[(lv + 1) * c:(lv + 2) * c, :] * lax.dot_general(
                q_lv, k_lv, nt, preferred_element_type=F32)
    for h in heads:
        o[h] = o[h] + jnp.dot(att[h].astype(BF16), v_ref[:, vc[h]], preferred_element_type=F32)
    for h in heads:
        st_ref[h] = st_ref[h] * decay[h][c - 1:c, :] + lax.dot_general(
            v_ref[:, vc[h]], (kf[h] * decay[h][c:2 * c]).astype(BF16), (((0,), (0,)), ((), ())),
            preferred_element_type=F32)
    for h in heads:
        y = o[h] * lax.rsqrt(jnp.mean(o[h] * o[h], axis=-1, keepdims=True) + LN_EPS) * ng_ref[...]
        rg = r_ref[:, vc[h]].astype(F32)
        o_ref[:, vc[h]] = (y * (rg * jax.nn.sigmoid(rg))).astype(BF16)


def _gla_core(qkv, r, g_low, w_g2, b_g2, norm_g):
    c = GLA_CHUNK
    sums, masks = _gla_tables()
    fixed = lambda n: (0, 0)
    return pl.pallas_call(
        _gla_kernel,
        grid=(L_PAD // c,),
        in_specs=[
            pl.BlockSpec((c, GLA_DK), lambda n: (n, 0)),
            pl.BlockSpec((c, GLA_DK), lambda n: (n, 1)),
            pl.BlockSpec((c, GLA_DV), lambda n: (n, 1)),
            pl.BlockSpec((c, GLA_DV), lambda n: (n, 0)),
            pl.BlockSpec((c, LANE), lambda n: (n, 0)),
            pl.BlockSpec((LANE, GLA_DK), fixed),
            pl.BlockSpec((1, GLA_DK), fixed),
            pl.BlockSpec((1, GLA_HV), fixed),
            pl.BlockSpec(sums.shape, fixed),
            pl.BlockSpec(masks.shape, fixed),
        ],
        out_specs=pl.BlockSpec((c, GLA_DV), lambda n: (n, 0)),
        out_shape=jax.ShapeDtypeStruct((L_PAD, GLA_DV), BF16),
        scratch_shapes=[pltpu.VMEM((GLA_HEADS, GLA_HV, GLA_HK), F32)],
        compiler_params=_cparams("arbitrary"),
        name="gla_core",
    )(qkv, qkv, qkv, r, g_low, w_g2, b_g2, norm_g,
      jnp.asarray(sums, BF16), jnp.asarray(masks, F32))


def _cum_gate_kernel(fl_ref, bf_ref, o_ref, carry_ref):
    t = ROW_TILE

    @pl.when(pl.program_id(0) == 0)
    def _():
        carry_ref[...] = jnp.zeros_like(carry_ref)

    log_f = _log_sigmoid(fl_ref[...] + bf_ref[...]) * LOG2_E
    rid = lax.broadcasted_iota(jnp.int32, (t, t), 0)
    cid = lax.broadcasted_iota(jnp.int32, (t, t), 1)
    upper = jnp.where(rid <= cid, 1.0, 0.0).astype(BF16)
    cum = sum(lax.dot_general(p, upper, (((0,), (0,)), ((), ())), preferred_element_type=F32)
              for p in _split3(log_f))
    cum = cum + carry_ref[...]
    o_ref[...] = cum
    carry_ref[...] = cum[:, t - 1:t]


def _cum_gate(f_logit, bias):
    return pl.pallas_call(
        _cum_gate_kernel,
        grid=(N_ROW_TILES,),
        in_specs=[pl.BlockSpec((ROW_TILE, LANE), lambda i: (i, 0)),
                  pl.BlockSpec((1, LANE), lambda i: (0, 0))],
        out_specs=pl.BlockSpec((LANE, ROW_TILE), lambda i: (0, i)),
        out_shape=jax.ShapeDtypeStruct((LANE, L_PAD), F32),
        scratch_shapes=[pltpu.VMEM((LANE, 1), F32)],
        compiler_params=_cparams("arbitrary"),
        name="cum_forget_gate",
    )(f_logit, bias)


def _fox_kernel(q_ref, k_ref, v_ref, og_ref, c_ref, o_ref, m_sc, acc_sc):
    t = ROW_TILE
    qi = pl.program_id(1)
    rid = lax.broadcasted_iota(jnp.int32, (t, t), 0)
    cid = lax.broadcasted_iota(jnp.int32, (t, t), 1)
    causal = cid <= rid
    heads = range(FOX_GROUP)
    cols = [slice(g * FOX_HD, (g + 1) * FOX_HD) for g in heads]
    q_start = pl.multiple_of(qi * t, LANE)
    c_q0 = [c_ref[g, :, pl.ds(q_start, LANE)][:, 0:1] for g in heads]

    ones = jnp.ones((t, FOX_HD), BF16)

    def score(g, start):
        return (lax.dot_general(q_ref[:, cols[g]], k_ref[pl.ds(start, t), cols[g]],
                                (((1,), (1,)), ((), ())), preferred_element_type=F32)
                + (c_q0[g] - c_ref[g, :, pl.ds(start, t)]))

    def softmax(g, s, mask):
        if mask:
            s = jnp.where(causal, s, NEG_BIG)
        m = m_sc[g]
        m_new = jnp.maximum(m, jnp.max(s, axis=-1, keepdims=True))
        m_sc[g] = m_new
        p = jnp.exp2(s - jnp.concatenate([m_new] * (t // LANE), axis=1)).astype(BF16)
        return p, jnp.exp2(m - m_new)

    def accumulate(g, start, p, alpha):
        v_aug = jnp.concatenate([v_ref[pl.ds(start, t), cols[g]], ones], axis=1)
        acc_sc[g] = jnp.concatenate([alpha, alpha], axis=1) * acc_sc[g] + jnp.dot(
            p, v_aug, preferred_element_type=F32)

    def block(kb, mask):
        start = pl.multiple_of(kb * t, LANE)
        s = [score(g, start) for g in heads]
        pa = [softmax(g, s[g], mask) for g in heads]
        for g in heads:
            accumulate(g, start, *pa[g])

    for g in heads:
        m_sc[g] = jnp.full((t, LANE), NEG_BIG, F32)
        acc_sc[g] = jnp.zeros((t, 2 * FOX_HD), F32)

    def body(kb, _):
        block(kb, False)
        return 0

    lax.fori_loop(0, qi, body, 0)
    block(qi, True)
    for g in heads:
        acc = acc_sc[g]
        og = og_ref[:, cols[g]].astype(F32)
        o_ref[:, cols[g]] = (acc[:, :FOX_HD] / acc[:, FOX_HD:] * jax.nn.sigmoid(og)).astype(BF16)


def _fox_attention(q_og, kv, c_t):
    w = FOX_GROUP * FOX_HD
    n_groups = FOX_HEADS // FOX_GROUP
    return pl.pallas_call(
        _fox_kernel,
        grid=(n_groups, N_ROW_TILES),
        in_specs=[
            pl.BlockSpec((ROW_TILE, w), lambda hg, qi: (qi, hg)),
            pl.BlockSpec((L_PAD, w), lambda hg, qi: (0, hg), pipeline_mode=pl.Buffered(1)),
            pl.BlockSpec((L_PAD, w), lambda hg, qi: (0, n_groups + hg),
                         pipeline_mode=pl.Buffered(1)),
            pl.BlockSpec((ROW_TILE, w), lambda hg, qi: (qi, n_groups + hg)),
            pl.BlockSpec((FOX_GROUP, 1, L_PAD), lambda hg, qi: (hg, 0, 0)),
        ],
        out_specs=pl.BlockSpec((ROW_TILE, w), lambda hg, qi: (qi, hg)),
        out_shape=jax.ShapeDtypeStruct((L_PAD, D_MODEL), BF16),
        scratch_shapes=[pltpu.VMEM((FOX_GROUP, ROW_TILE, LANE), F32),
                        pltpu.VMEM((FOX_GROUP, ROW_TILE, 2 * FOX_HD), F32)],
        compiler_params=_cparams("parallel", "parallel"),
        name="fox_attention",
    )(q_og, kv, kv, q_og, c_t)


def _pad_cols(w, n):
    return jnp.pad(w, ((0, 0), (0, n - w.shape[1])))


def kernel(x, meta, ln_g, ln_b, gla_w_in, gla_w_g2, gla_b_g2, gla_norm_g, gla_w_out, kv_w, kv_bf,
           fox_w_in, fox_w_out, ffn_w_up, ffn_conv_w, ffn_conv_b, ffn_w_down):
    assert x.shape == (1, SEQ, D_MODEL)
    tail = jnp.zeros((L_PAD - L_VALID, D_MODEL), F32)
    hf = jnp.concatenate([meta.astype(F32), x[0], tail], axis=0)
    hb = hf.astype(BF16)

    ones = lambda n: jnp.ones((1, n), F32)
    row = lambda v: v.reshape(1, -1)
    g_lo, g_hi = 2 * GLA_DK + GLA_DV, 2 * GLA_DK + GLA_DV + GLA_RANK

    gla_in = gla_w_in.astype(BF16)
    gla_r = gla_in[:, :, g_hi:]
    gla_g1 = jnp.pad(gla_in[:, :, g_lo:g_hi], ((0, 0), (0, 0), (0, LANE - GLA_RANK)))
    gla_g2 = jnp.pad(gla_w_g2, ((0, 0), (0, LANE - GLA_RANK), (0, 0))).astype(BF16)
    gla_out = gla_w_out.astype(BF16)
    kv_in = kv_w.astype(BF16)[None]
    kv_f = _pad_cols(kv_w[:, 2 * D_MODEL:], LANE).astype(BF16)[None]
    fox_in = fox_w_in.astype(BF16)
    fox_out = fox_w_out.astype(BF16)
    ffn_up = ffn_w_up.astype(BF16)
    ffn_down = ffn_w_down.astype(BF16)
    ffn_cb = ffn_conv_b[:, None, :]
    gla_q_scale = jnp.concatenate([jnp.full((1, GLA_DK), GLA_HK ** -0.5, F32), ones(g_lo - GLA_DK)],
                                  axis=1)
    fox_q_scale = jnp.concatenate([jnp.full((1, D_MODEL), FOX_HD ** -0.5 * LOG2_E, F32),
                                   ones(D_MODEL)], axis=1)

    c_t = kv = None
    for l in range(DEPTH):
        if l < N_A_LAYERS:
            qkv = _matmul(hb, gla_in, l, g_lo, gla_q_scale, BF16, PROJ_TILE)
            r = _matmul(hb, gla_r, l, GLA_DV, ones(GLA_DV), BF16, PROJ_TILE)
            g_low = _matmul(hb, gla_g1, l, LANE, ones(LANE), F32, LANE)
            mix = _gla_core(qkv, r, g_low, gla_g2[l], row(gla_b_g2[l]), row(gla_norm_g[l]))
            w_out, j = gla_out, l
        else:
            j = l - N_A_LAYERS
            if j == 0:
                kv = _matmul(hb, kv_in, 0, 2 * D_MODEL, ones(2 * D_MODEL), BF16, PROJ_TILE)
                f_logit = _matmul(hb, kv_f, 0, LANE, ones(LANE), F32, LANE)
                c_t = _cum_gate(f_logit, _pad_cols(row(kv_bf), LANE))[:FOX_HEADS, None, :]
            q_og = _matmul(hb, fox_in, j, 2 * D_MODEL, fox_q_scale, BF16, PROJ_TILE)
            mix = _fox_attention(q_og, kv, c_t)
            w_out = fox_out
        hf, hb = _proj_residual_ln(mix, w_out, j, hf, row(ln_g[l, 0]), row(ln_b[l, 0]))
        hf, hb = _conv_ffn_ln(hb, hf, ffn_up, ffn_conv_w, ffn_cb, ffn_down, l,
                              row(ln_g[l, 1]), row(ln_b[l, 1]))
    return hf[N_META:L_VALID][None]
```

```python
import jax
import jax.numpy as jnp
import numpy as np
from jax import lax
from jax.experimental import pallas as pl
from jax.experimental.pallas import tpu as pltpu

F32 = jnp.float32
BF16 = jnp.bfloat16

D_MODEL = 2048
SEQ = 8192
DEPTH = 4
N_META = 16
N_A_LAYERS = DEPTH // 2
ALPHA = (2.0 * DEPTH) ** 0.25
LN_EPS = 1e-5

GLA_HEADS = 4
GLA_DK = D_MODEL // 2
GLA_DV = D_MODEL
GLA_HK = GLA_DK // GLA_HEADS
GLA_HV = GLA_DV // GLA_HEADS
GLA_RANK = 16
GLA_TAU = 16.0

FOX_HEADS = 16
FOX_HD = D_MODEL // FOX_HEADS

D_FF = 5632

LANE = 128
L_VALID = N_META + SEQ
ROW_TILE = 640
L_PAD = 13 * ROW_TILE
N_ROW_TILES = L_PAD // ROW_TILE

GLA_CHUNK = 64
PROJ_TILE = 1024
PROJ_ROWS = 13 * LANE
FF_TILE = 512
FF_SUB = 256
N_FF_TILES = D_FF // FF_TILE
FOX_GROUP = 4
FOX_KEY_TILES = 2
CONV_HALO = 8

VMEM_LIMIT = 56 * 1024 * 1024
NEG_BIG = -1e30
LOG2_E = 1.4426950408889634


def _cparams(*sem):
    return pltpu.CompilerParams(dimension_semantics=sem, vmem_limit_bytes=VMEM_LIMIT)


def _layer_norm(y, g, b):
    mu = jnp.mean(y, axis=-1, keepdims=True)
    d = y - mu
    var = jnp.mean(d * d, axis=-1, keepdims=True)
    return d * lax.rsqrt(var + LN_EPS) * g + b


def _log_sigmoid(z):
    return jnp.minimum(z, 0.0) - jnp.log1p(jnp.exp(-jnp.abs(z)))


def _split3(x):
    hi = x.astype(BF16)
    r1 = x - hi.astype(F32)
    mid = r1.astype(BF16)
    lo = (r1 - mid.astype(F32)).astype(BF16)
    return hi, mid, lo


def _mm_kernel(x_ref, w_ref, s_ref, o_ref):
    acc = jnp.dot(x_ref[...], w_ref[...], preferred_element_type=F32)
    o_ref[...] = (acc * s_ref[...]).astype(o_ref.dtype)


def _matmul(x, w, layer, n, col_scale, out_dtype, tn):
    k = x.shape[1]
    return pl.pallas_call(
        _mm_kernel,
        grid=(L_PAD // PROJ_ROWS, n // tn),
        in_specs=[
            pl.BlockSpec((PROJ_ROWS, k), lambda i, j: (i, 0)),
            pl.BlockSpec((None, k, tn), lambda i, j: (layer, 0, j)),
            pl.BlockSpec((1, tn), lambda i, j: (0, j)),
        ],
        out_specs=pl.BlockSpec((PROJ_ROWS, tn), lambda i, j: (i, j)),
        out_shape=jax.ShapeDtypeStruct((L_PAD, n), out_dtype),
        compiler_params=_cparams("parallel", "parallel"),
        name="proj_matmul",
    )(x, w, col_scale)


def _proj_ln_kernel(x_ref, w_ref, h_ref, g_ref, b_ref, of_ref, ob_ref):
    half = ROW_TILE // 2
    for rows in (slice(0, half), slice(half, ROW_TILE)):
        acc = jnp.dot(x_ref[rows, :], w_ref[...], preferred_element_type=F32)
        out = _layer_norm(ALPHA * h_ref[rows, :] + acc, g_ref[...], b_ref[...])
        of_ref[rows, :] = out
        ob_ref[rows, :] = out.astype(BF16)


def _proj_residual_ln(x, w, layer, h, g, b):
    row = lambda i: (i, 0)
    fixed = lambda i: (0, 0)
    return pl.pallas_call(
        _proj_ln_kernel,
        grid=(N_ROW_TILES,),
        in_specs=[
            pl.BlockSpec((ROW_TILE, D_MODEL), row),
            pl.BlockSpec((None, D_MODEL, D_MODEL), lambda i: (layer, 0, 0)),
            pl.BlockSpec((ROW_TILE, D_MODEL), row),
            pl.BlockSpec((1, D_MODEL), fixed),
            pl.BlockSpec((1, D_MODEL), fixed),
        ],
        out_specs=[pl.BlockSpec((ROW_TILE, D_MODEL), row), pl.BlockSpec((ROW_TILE, D_MODEL), row)],
        out_shape=[jax.ShapeDtypeStruct((L_PAD, D_MODEL), F32),
                   jax.ShapeDtypeStruct((L_PAD, D_MODEL), BF16)],
        compiler_params=_cparams("parallel"),
        name="out_proj_ln",
    )(x, w, h, g, b)


def _ffn_kernel(xb_ref, xf_ref, wu_ref, wg_ref, cwu_ref, cwg_ref, cbu_ref, cbg_ref, wd_ref,
                g_ref, b_ref, of_ref, ob_ref, ext_ref, carry_ref):
    i = pl.program_id(0)
    j = pl.program_id(1)

    @pl.when(i == 0)
    def _():
        carry_ref[j] = jnp.zeros(carry_ref.shape[1:], F32)

    @pl.when(j == 0)
    def _():
        of_ref[...] = jnp.zeros_like(of_ref)

    xb = xb_ref[...]

    def up(slot, cols, w_ref):
        hval = jnp.dot(xb, w_ref[:, cols], preferred_element_type=F32)
        ext_ref[slot, 0:CONV_HALO, :] = carry_ref[j, slot]
        ext_ref[slot, CONV_HALO:CONV_HALO + ROW_TILE, :] = hval
        carry_ref[j, slot] = hval[ROW_TILE - CONV_HALO:, :]
        return hval

    def conv(slot, cols, hval, cw_ref, cb_ref):
        cw = cw_ref[:, cols]
        return (cw[2:3, :] * hval
                + cw[1:2, :] * ext_ref[slot, CONV_HALO - 1:CONV_HALO - 1 + ROW_TILE, :]
                + cw[0:1, :] * ext_ref[slot, CONV_HALO - 2:CONV_HALO - 2 + ROW_TILE, :]
                + cb_ref[:, cols])

    chunks = [slice(c * FF_SUB, (c + 1) * FF_SUB) for c in range(FF_TILE // FF_SUB)]
    hu = [up(2 * c, cols, wu_ref) for c, cols in enumerate(chunks)]
    hg = [up(2 * c + 1, cols, wg_ref) for c, cols in enumerate(chunks)]
    contrib = None
    for c, cols in enumerate(chunks):
        u = conv(2 * c, cols, hu[c], cwu_ref, cbu_ref)
        gt = conv(2 * c + 1, cols, hg[c], cwg_ref, cbg_ref)
        act = (gt * jax.nn.sigmoid(gt) * u).astype(BF16)
        part = jnp.dot(act, wd_ref[cols, :], preferred_element_type=F32)
        contrib = part if contrib is None else contrib + part
    of_ref[...] += contrib

    @pl.when(j == N_FF_TILES - 1)
    def _():
        out = _layer_norm(ALPHA * xf_ref[...] + of_ref[...], g_ref[...], b_ref[...])
        of_ref[...] = out
        ob_ref[...] = out.astype(BF16)


def _conv_ffn_ln(hb, hf, w_up, conv_w, conv_b, w_down, layer, g, b):
    row = lambda i, j: (i, 0)
    fixed = lambda i, j: (0, 0)
    ucol = lambda i, j: (layer, 0, j)
    gcol = lambda i, j: (layer, 0, N_FF_TILES + j)
    n_slots = 2 * FF_TILE // FF_SUB
    return pl.pallas_call(
        _ffn_kernel,
        grid=(N_ROW_TILES, N_FF_TILES),
        in_specs=[
            pl.BlockSpec((ROW_TILE, D_MODEL), row, pipeline_mode=pl.Buffered(1)),
            pl.BlockSpec((ROW_TILE, D_MODEL), row, pipeline_mode=pl.Buffered(1)),
            pl.BlockSpec((None, D_MODEL, FF_TILE), ucol),
            pl.BlockSpec((None, D_MODEL, FF_TILE), gcol),
            pl.BlockSpec((None, 3, FF_TILE), ucol),
            pl.BlockSpec((None, 3, FF_TILE), gcol),
            pl.BlockSpec((None, 1, FF_TILE), ucol),
            pl.BlockSpec((None, 1, FF_TILE), gcol),
            pl.BlockSpec((None, FF_TILE, D_MODEL), lambda i, j: (layer, j, 0)),
            pl.BlockSpec((1, D_MODEL), fixed),
            pl.BlockSpec((1, D_MODEL), fixed),
        ],
        out_specs=[pl.BlockSpec((ROW_TILE, D_MODEL), row), pl.BlockSpec((ROW_TILE, D_MODEL), row)],
        out_shape=[jax.ShapeDtypeStruct((L_PAD, D_MODEL), F32),
                   jax.ShapeDtypeStruct((L_PAD, D_MODEL), BF16)],
        scratch_shapes=[
            pltpu.VMEM((n_slots, ROW_TILE + CONV_HALO, FF_SUB), F32),
            pltpu.VMEM((N_FF_TILES, n_slots, CONV_HALO, FF_SUB), F32),
        ],
        compiler_params=_cparams("arbitrary", "arbitrary"),
        name="conv_ffn_ln",
    )(hb, hf, w_up, w_up, conv_w, conv_w, conv_b, conv_b, w_down, g, b)


def _gla_tables():
    c = GLA_CHUNK
    i = np.arange(c)[:, None]
    t = np.arange(c)[None, :]
    sums = [t <= i, t > i]
    masks = [i == t]
    s = c // 2
    while s >= 1:
        blk = i // s
        boundary = blk * s
        sums.append((blk % 2 == 1) & (t > boundary) & (t <= i))
        sums.append((blk % 2 == 0) & (t > i) & (t <= boundary + s))
        masks.append((blk % 2 == 1) & (t // s == blk - 1))
        s //= 2
    sums = np.concatenate(sums, axis=0).astype(np.float32)
    masks = np.concatenate(masks, axis=0).astype(np.float32)
    return np.concatenate([sums, sums, sums], axis=1), masks


def _gla_kernel(q_ref, k_ref, v_ref, r_ref, gl_ref, w2_ref, b2_ref, ng_ref, sum_ref, mask_ref,
                o_ref, st_ref):
    c = GLA_CHUNK
    n_levels = mask_ref.shape[0] // c - 1
    nt = (((1,), (1,)), ((), ()))

    @pl.when(pl.program_id(0) == 0)
    def _():
        st_ref[...] = jnp.zeros_like(st_ref)

    z = jnp.dot(gl_ref[...].astype(BF16), w2_ref[...], preferred_element_type=F32) + b2_ref[...]
    log2_a = _log_sigmoid(z) * (LOG2_E / GLA_TAU)
    sum_table = sum_ref[...]
    heads = range(GLA_HEADS)
    kc = [slice(h * GLA_HK, (h + 1) * GLA_HK) for h in heads]
    vc = [slice(h * GLA_HV, (h + 1) * GLA_HV) for h in heads]
    decay = [jnp.exp2(jnp.dot(sum_table, jnp.concatenate(_split3(log2_a[:, kc[h]]), axis=0),
                              preferred_element_type=F32)) for h in heads]
    qf = [q_ref[:, kc[h]].astype(F32) for h in heads]
    kf = [k_ref[:, kc[h]].astype(F32) for h in heads]
    o = [lax.dot_general((qf[h] * decay[h][0:c]).astype(BF16), st_ref[h].astype(BF16), nt,
                         preferred_element_type=F32) for h in heads]
    att = [mask_ref[0:c, :] * lax.dot_general(q_ref[:, kc[h]], k_ref[:, kc[h]], nt,
                                              preferred_element_type=F32) for h in heads]
    for lv in range(n_levels):
        r0 = (2 + 2 * lv) * c
        for h in heads:
            q_lv = (qf[h] * decay[h][r0:r0 + c]).astype(BF16)
            k_lv = (kf[h] * decay[h][r0 + c:r0 + 2 * c]).astype(BF16)
            att[h] = att[h] + mask_ref[(lv + 1) * c:(lv + 2) * c, :] * lax.dot_general(
                q_lv, k_lv, nt, preferred_element_type=F32)
    for h in heads:
        o[h] = o[h] + jnp.dot(att[h].astype(BF16), v_ref[:, vc[h]], preferred_element_type=F32)
    for h in heads:
        st_ref[h] = st_ref[h] * decay[h][c - 1:c, :] + lax.dot_general(
            v_ref[:, vc[h]], (kf[h] * decay[h][c:2 * c]).astype(BF16), (((0,), (0,)), ((), ())),
            preferred_element_type=F32)
    for h in heads:
        y = o[h] * lax.rsqrt(jnp.mean(o[h] * o[h], axis=-1, keepdims=True) + LN_EPS) * ng_ref[...]
        rg = r_ref[:, vc[h]].astype(F32)
        o_ref[:, vc[h]] = (y * (rg * jax.nn.sigmoid(rg))).astype(BF16)


def _gla_core(qkv, r, g_low, w_g2, b_g2, norm_g):
    c = GLA_CHUNK
    sums, masks = _gla_tables()
    fixed = lambda n: (0, 0)
    return pl.pallas_call(
        _gla_kernel,
        grid=(L_PAD // c,),
        in_specs=[
            pl.BlockSpec((c, GLA_DK), lambda n: (n, 0)),
            pl.BlockSpec((c, GLA_DK), lambda n: (n, 1)),
            pl.BlockSpec((c, GLA_DV), lambda n: (n, 1)),
            pl.BlockSpec((c, GLA_DV), lambda n: (n, 0)),
            pl.BlockSpec((c, LANE), lambda n: (n, 0)),
            pl.BlockSpec((LANE, GLA_DK), fixed),
            pl.BlockSpec((1, GLA_DK), fixed),
            pl.BlockSpec((1, GLA_HV), fixed),
            pl.BlockSpec(sums.shape, fixed),
            pl.BlockSpec(masks.shape, fixed),
        ],
        out_specs=pl.BlockSpec((c, GLA_DV), lambda n: (n, 0)),
        out_shape=jax.ShapeDtypeStruct((L_PAD, GLA_DV), BF16),
        scratch_shapes=[pltpu.VMEM((GLA_HEADS, GLA_HV, GLA_HK), F32)],
        compiler_params=_cparams("arbitrary"),
        name="gla_core",
    )(qkv, qkv, qkv, r, g_low, w_g2, b_g2, norm_g,
      jnp.asarray(sums, BF16), jnp.asarray(masks, F32))


def _cum_gate_kernel(fl_ref, bf_ref, o_ref, carry_ref):
    t = ROW_TILE

    @pl.when(pl.program_id(0) == 0)
    def _():
        carry_ref[...] = jnp.zeros_like(carry_ref)

    log_f = _log_sigmoid(fl_ref[...] + bf_ref[...]) * LOG2_E
    rid = lax.broadcasted_iota(jnp.int32, (t, t), 0)
    cid = lax.broadcasted_iota(jnp.int32, (t, t), 1)
    upper = jnp.where(rid <= cid, 1.0, 0.0).astype(BF16)
    cum = sum(lax.dot_general(p, upper, (((0,), (0,)), ((), ())), preferred_element_type=F32)
              for p in _split3(log_f))
    cum = cum + carry_ref[...]
    o_ref[...] = cum
    carry_ref[...] = cum[:, t - 1:t]


def _cum_gate(f_logit, bias):
    return pl.pallas_call(
        _cum_gate_kernel,
        grid=(N_ROW_TILES,),
        in_specs=[pl.BlockSpec((ROW_TILE, LANE), lambda i: (i, 0)),
                  pl.BlockSpec((1, LANE), lambda i: (0, 0))],
        out_specs=pl.BlockSpec((LANE, ROW_TILE), lambda i: (0, i)),
        out_shape=jax.ShapeDtypeStruct((LANE, L_PAD), F32),
        scratch_shapes=[pltpu.VMEM((LANE, 1), F32)],
        compiler_params=_cparams("arbitrary"),
        name="cum_forget_gate",
    )(f_logit, bias)


def _fox_kernel(q_ref, k_ref, v_ref, og_ref, c_ref, o_ref, m_sc, acc_sc):
    t = ROW_TILE
    qi = pl.program_id(1)
    rid = lax.broadcasted_iota(jnp.int32, (t, t), 0)
    cid = lax.broadcasted_iota(jnp.int32, (t, t), 1)
    causal = cid <= rid
    heads = range(FOX_GROUP)
    cols = [slice(g * FOX_HD, (g + 1) * FOX_HD) for g in heads]
    q_start = pl.multiple_of(qi * t, LANE)
    c_q0 = [c_ref[g, :, pl.ds(q_start, LANE)][:, 0:1] for g in heads]

    def score(g, start, w):
        return (lax.dot_general(q_ref[:, cols[g]], k_ref[pl.ds(start, w), cols[g]],
                                (((1,), (1,)), ((), ())), preferred_element_type=F32)
                + (c_q0[g] - c_ref[g, :, pl.ds(start, w)]))

    def softmax(g, s, mask):
        if mask:
            s = jnp.where(causal, s, NEG_BIG)
        m = m_sc[g]
        m_new = jnp.maximum(m, jnp.max(s, axis=-1, keepdims=True))
        m_sc[g] = m_new
        p = jnp.exp2(s - jnp.concatenate([m_new] * (s.shape[1] // LANE), axis=1)).astype(BF16)
        return p, jnp.exp2(m - m_new)

    def accumulate(g, start, p, alpha):
        w = p.shape[1]
        v_aug = jnp.concatenate([v_ref[pl.ds(start, w), cols[g]], jnp.ones((w, FOX_HD), BF16)],
                                axis=1)
        acc_sc[g] = jnp.concatenate([alpha, alpha], axis=1) * acc_sc[g] + jnp.dot(
            p, v_aug, preferred_element_type=F32)

    def block(start, w, mask):
        start = pl.multiple_of(start, LANE)
        s = [score(g, start, w) for g in heads]
        pa = [softmax(g, s[g], mask) for g in heads]
        for g in heads:
            accumulate(g, start, *pa[g])

    for g in heads:
        m_sc[g] = jnp.full((t, LANE), NEG_BIG, F32)
        acc_sc[g] = jnp.zeros((t, 2 * FOX_HD), F32)

    wide = FOX_KEY_TILES * t

    def wide_body(kb, _):
        block(kb * wide, wide, False)
        return 0

    def narrow_body(kb, _):
        block(kb * t, t, False)
        return 0

    n_wide = qi // FOX_KEY_TILES
    lax.fori_loop(0, n_wide, wide_body, 0)
    lax.fori_loop(n_wide * FOX_KEY_TILES, qi, narrow_body, 0)
    block(qi * t, t, True)
    for g in heads:
        acc = acc_sc[g]
        og = og_ref[:, cols[g]].astype(F32)
        o_ref[:, cols[g]] = (acc[:, :FOX_HD] / acc[:, FOX_HD:] * jax.nn.sigmoid(og)).astype(BF16)


def _fox_attention(q_og, kv, c_t):
    w = FOX_GROUP * FOX_HD
    n_groups = FOX_HEADS // FOX_GROUP
    return pl.pallas_call(
        _fox_kernel,
        grid=(n_groups, N_ROW_TILES),
        in_specs=[
            pl.BlockSpec((ROW_TILE, w), lambda hg, qi: (qi, hg)),
            pl.BlockSpec((L_PAD, w), lambda hg, qi: (0, hg), pipeline_mode=pl.Buffered(1)),
            pl.BlockSpec((L_PAD, w), lambda hg, qi: (0, n_groups + hg),
                         pipeline_mode=pl.Buffered(1)),
            pl.BlockSpec((ROW_TILE, w), lambda hg, qi: (qi, n_groups + hg)),
            pl.BlockSpec((FOX_GROUP, 1, L_PAD), lambda hg, qi: (hg, 0, 0)),
        ],
        out_specs=pl.BlockSpec((ROW_TILE, w), lambda hg, qi: (qi, hg)),
        out_shape=jax.ShapeDtypeStruct((L_PAD, D_MODEL), BF16),
        scratch_shapes=[pltpu.VMEM((FOX_GROUP, ROW_TILE, LANE), F32),
                        pltpu.VMEM((FOX_GROUP, ROW_TILE, 2 * FOX_HD), F32)],
        compiler_params=_cparams("parallel", "parallel"),
        name="fox_attention",
    )(q_og, kv, kv, q_og, c_t)


def _pad_cols(w, n):
    return jnp.pad(w, ((0, 0), (0, n - w.shape[1])))


def kernel(x, meta, ln_g, ln_b, gla_w_in, gla_w_g2, gla_b_g2, gla_norm_g, gla_w_out, kv_w, kv_bf,
           fox_w_in, fox_w_out, ffn_w_up, ffn_conv_w, ffn_conv_b, ffn_w_down):
    assert x.shape == (1, SEQ, D_MODEL)
    tail = jnp.zeros((L_PAD - L_VALID, D_MODEL), F32)
    hf = jnp.concatenate([meta.astype(F32), x[0], tail], axis=0)
    hb = hf.astype(BF16)

    ones = lambda n: jnp.ones((1, n), F32)
    row = lambda v: v.reshape(1, -1)
    g_lo, g_hi = 2 * GLA_DK + GLA_DV, 2 * GLA_DK + GLA_DV + GLA_RANK

    gla_in = gla_w_in[:, :, :g_lo].astype(BF16)
    gla_r = gla_w_in[:, :, g_hi:].astype(BF16)
    gla_g1 = jnp.pad(gla_w_in[:, :, g_lo:g_hi],
                     ((0, 0), (0, 0), (0, LANE - GLA_RANK))).astype(BF16)
    gla_g2 = jnp.pad(gla_w_g2, ((0, 0), (0, LANE - GLA_RANK), (0, 0))).astype(BF16)
    gla_out = gla_w_out.astype(BF16)
    kv_in = kv_w[:, :2 * D_MODEL].astype(BF16)[None]
    kv_f = _pad_cols(kv_w[:, 2 * D_MODEL:], LANE).astype(BF16)[None]
    fox_in = fox_w_in.astype(BF16)
    fox_out = fox_w_out.astype(BF16)
    ffn_up = ffn_w_up.astype(BF16)
    ffn_down = ffn_w_down.astype(BF16)
    ffn_cb = ffn_conv_b[:, None, :]
    gla_q_scale = jnp.concatenate([jnp.full((1, GLA_DK), GLA_HK ** -0.5, F32), ones(g_lo - GLA_DK)],
                                  axis=1)
    fox_q_scale = jnp.concatenate([jnp.full((1, D_MODEL), FOX_HD ** -0.5 * LOG2_E, F32),
                                   ones(D_MODEL)], axis=1)

    c_t = kv = None
    for l in range(DEPTH):
        if l < N_A_LAYERS:
            qkv = _matmul(hb, gla_in, l, g_lo, gla_q_scale, BF16, PROJ_TILE)
            r = _matmul(hb, gla_r, l, GLA_DV, ones(GLA_DV), BF16, PROJ_TILE)
            g_low = _matmul(hb, gla_g1, l, LANE, ones(LANE), F32, LANE)
            mix = _gla_core(qkv, r, g_low, gla_g2[l], row(gla_b_g2[l]), row(gla_norm_g[l]))
            w_out, j = gla_out, l
        else:
            j = l - N_A_LAYERS
            if j == 0:
                kv = _matmul(hb, kv_in, 0, 2 * D_MODEL, ones(2 * D_MODEL), BF16, PROJ_TILE)
                f_logit = _matmul(hb, kv_f, 0, LANE, ones(LANE), F32, LANE)
                c_t = _cum_gate(f_logit, _pad_cols(row(kv_bf), LANE))[:FOX_HEADS, None, :]
            q_og = _matmul(hb, fox_in, j, 2 * D_MODEL, fox_q_scale, BF16, PROJ_TILE)
            mix = _fox_attention(q_og, kv, c_t)
            w_out = fox_out
        hf, hb = _proj_residual_ln(mix, w_out, j, hf, row(ln_g[l, 0]), row(ln_b[l, 0]))
        hf, hb = _conv_ffn_ln(hb, hf, ffn_up, ffn_conv_w, ffn_cb, ffn_down, l,
                              row(ln_g[l, 1]), row(ln_b[l, 1]))
    return hf[N_META:L_VALID][None]
```

```python
import jax
import jax.numpy as jnp
import numpy as np
from jax import lax
from jax.experimental import pallas as pl
from jax.experimental.pallas import tpu as pltpu

F32 = jnp.float32
BF16 = jnp.bfloat16

D_MODEL = 2048
SEQ = 8192
DEPTH = 4
N_META = 16
N_A_LAYERS = DEPTH // 2
ALPHA = (2.0 * DEPTH) ** 0.25
LN_EPS = 1e-5

GLA_HEADS = 4
GLA_DK = D_MODEL // 2
GLA_DV = D_MODEL
GLA_HK = GLA_DK // GLA_HEADS
GLA_HV = GLA_DV // GLA_HEADS
GLA_RANK = 16
GLA_TAU = 16.0

FOX_HEADS = 16
FOX_HD = D_MODEL // FOX_HEADS

D_FF = 5632

LANE = 128
L_VALID = N_META + SEQ
ROW_TILE = 640
L_PAD = 13 * ROW_TILE
N_ROW_TILES = L_PAD // ROW_TILE

GLA_CHUNK = 64
PROJ_TILE = 1024
PROJ_ROWS = 13 * LANE
FF_TILE = 512
FF_SUB = 256
FF_ROW_PIECES = 2
N_FF_TILES = D_FF // FF_TILE
FOX_GROUP = 4
FOX_KEY_TILES = 2
CONV_HALO = 8

VMEM_LIMIT = 56 * 1024 * 1024
NEG_BIG = -1e30
LOG2_E = 1.4426950408889634


def _cparams(*sem):
    return pltpu.CompilerParams(dimension_semantics=sem, vmem_limit_bytes=VMEM_LIMIT)


def _layer_norm(y, g, b):
    mu = jnp.mean(y, axis=-1, keepdims=True)
    d = y - mu
    var = jnp.mean(d * d, axis=-1, keepdims=True)
    return d * lax.rsqrt(var + LN_EPS) * g + b


def _log_sigmoid(z):
    return jnp.minimum(z, 0.0) - jnp.log1p(jnp.exp(-jnp.abs(z)))


def _split3(x):
    hi = x.astype(BF16)
    r1 = x - hi.astype(F32)
    mid = r1.astype(BF16)
    lo = (r1 - mid.astype(F32)).astype(BF16)
    return hi, mid, lo


def _mm_kernel(x_ref, w_ref, s_ref, o_ref):
    tn = o_ref.shape[1]
    acc = jnp.dot(x_ref[...], w_ref[...], preferred_element_type=F32)
    scale = s_ref[:, pl.ds(pl.multiple_of(pl.program_id(1) * tn, LANE), tn)]
    o_ref[...] = (acc * scale).astype(o_ref.dtype)


def _matmul(x, w, layer, n, col_scale, out_dtype, tn):
    k = x.shape[1]
    return pl.pallas_call(
        _mm_kernel,
        grid=(L_PAD // PROJ_ROWS, n // tn),
        in_specs=[
            pl.BlockSpec((PROJ_ROWS, k), lambda i, j: (i, 0)),
            pl.BlockSpec((None, k, tn), lambda i, j: (layer, 0, j)),
            pl.BlockSpec((1, n), lambda i, j: (0, 0)),
        ],
        out_specs=pl.BlockSpec((PROJ_ROWS, tn), lambda i, j: (i, j)),
        out_shape=jax.ShapeDtypeStruct((L_PAD, n), out_dtype),
        compiler_params=_cparams("parallel", "parallel"),
        name="proj_matmul",
    )(x, w, col_scale)


def _proj_ln_kernel(x_ref, w_ref, h_ref, g_ref, b_ref, of_ref, ob_ref):
    half = ROW_TILE // 2
    for rows in (slice(0, half), slice(half, ROW_TILE)):
        acc = jnp.dot(x_ref[rows, :], w_ref[...], preferred_element_type=F32)
        out = _layer_norm(ALPHA * h_ref[rows, :] + acc, g_ref[...], b_ref[...])
        of_ref[rows, :] = out
        ob_ref[rows, :] = out.astype(BF16)


def _proj_residual_ln(x, w, layer, h, g, b):
    row = lambda i: (i, 0)
    fixed = lambda i: (0, 0)
    return pl.pallas_call(
        _proj_ln_kernel,
        grid=(N_ROW_TILES,),
        in_specs=[
            pl.BlockSpec((ROW_TILE, D_MODEL), row),
            pl.BlockSpec((None, D_MODEL, D_MODEL), lambda i: (layer, 0, 0)),
            pl.BlockSpec((ROW_TILE, D_MODEL), row),
            pl.BlockSpec((1, D_MODEL), fixed),
            pl.BlockSpec((1, D_MODEL), fixed),
        ],
        out_specs=[pl.BlockSpec((ROW_TILE, D_MODEL), row), pl.BlockSpec((ROW_TILE, D_MODEL), row)],
        out_shape=[jax.ShapeDtypeStruct((L_PAD, D_MODEL), F32),
                   jax.ShapeDtypeStruct((L_PAD, D_MODEL), BF16)],
        compiler_params=_cparams("parallel"),
        name="out_proj_ln",
    )(x, w, h, g, b)


def _ffn_kernel(xb_ref, xf_ref, wu_ref, wg_ref, cw_ref, cb_ref, wd_ref,
                g_ref, b_ref, of_ref, ob_ref, ext_ref, carry_ref):
    i = pl.program_id(0)
    j = pl.program_id(1)

    @pl.when(i == 0)
    def _():
        carry_ref[j] = jnp.zeros(carry_ref.shape[1:], F32)

    @pl.when(j == 0)
    def _():
        of_ref[...] = jnp.zeros_like(of_ref)

    piece = ROW_TILE // FF_ROW_PIECES
    pieces = [slice(r * piece, (r + 1) * piece) for r in range(FF_ROW_PIECES)]
    n_chunks = FF_TILE // FF_SUB
    chunks = [slice(c * FF_SUB, (c + 1) * FF_SUB) for c in range(n_chunks)]

    def up(c, r):
        rows = pieces[r]
        for slot, w_ref in ((2 * c, wu_ref), (2 * c + 1, wg_ref)):
            hval = jnp.dot(xb_ref[rows, :], w_ref[:, chunks[c]], preferred_element_type=F32)
            if r == 0:
                ext_ref[slot, 0:CONV_HALO, :] = carry_ref[j, slot]
            if r == FF_ROW_PIECES - 1:
                carry_ref[j, slot] = hval[piece - CONV_HALO:, :]
            ext_ref[slot, CONV_HALO + rows.start:CONV_HALO + rows.stop, :] = hval

    def gate(c, r):
        def conv(slot, first_col):
            cols = pl.ds(pl.multiple_of(first_col, LANE), FF_SUB)
            cw = cw_ref[:, cols]
            lo = CONV_HALO + pieces[r].start
            return (cw[2:3, :] * ext_ref[slot, lo:lo + piece, :]
                    + cw[1:2, :] * ext_ref[slot, lo - 1:lo - 1 + piece, :]
                    + cw[0:1, :] * ext_ref[slot, lo - 2:lo - 2 + piece, :]
                    + cb_ref[:, cols])
        u = conv(2 * c, j * FF_TILE + c * FF_SUB)
        gt = conv(2 * c + 1, D_FF + j * FF_TILE + c * FF_SUB)
        return (gt * jax.nn.sigmoid(gt) * u).astype(BF16)

    for r in range(FF_ROW_PIECES):
        up(0, r)
    acts = [[None] * FF_ROW_PIECES for _ in range(n_chunks)]
    for c in range(n_chunks):
        for r in range(FF_ROW_PIECES):
            acts[c][r] = gate(c, r)
            if c + 1 < n_chunks:
                up(c + 1, r)
            else:
                rows = pieces[r]
                part = None
                for cc in range(n_chunks):
                    d = jnp.dot(acts[cc][r], wd_ref[chunks[cc], :], preferred_element_type=F32)
                    part = d if part is None else part + d
                of_ref[rows, :] += part

    @pl.when(j == N_FF_TILES - 1)
    def _():
        out = _layer_norm(ALPHA * xf_ref[...] + of_ref[...], g_ref[...], b_ref[...])
        of_ref[...] = out
        ob_ref[...] = out.astype(BF16)


def _conv_ffn_ln(hb, hf, w_up, conv_w, conv_b, w_down, layer, g, b):
    row = lambda i, j: (i, 0)
    fixed = lambda i, j: (0, 0)
    ucol = lambda i, j: (layer, 0, j)
    gcol = lambda i, j: (layer, 0, N_FF_TILES + j)
    n_slots = 2 * FF_TILE // FF_SUB
    return pl.pallas_call(
        _ffn_kernel,
        grid=(N_ROW_TILES, N_FF_TILES),
        in_specs=[
            pl.BlockSpec((ROW_TILE, D_MODEL), row, pipeline_mode=pl.Buffered(1)),
            pl.BlockSpec((ROW_TILE, D_MODEL), row, pipeline_mode=pl.Buffered(1)),
            pl.BlockSpec((None, D_MODEL, FF_TILE), ucol),
            pl.BlockSpec((None, D_MODEL, FF_TILE), gcol),
            pl.BlockSpec((None, 3, 2 * D_FF), lambda i, j: (layer, 0, 0)),
            pl.BlockSpec((None, 1, 2 * D_FF), lambda i, j: (layer, 0, 0)),
            pl.BlockSpec((None, FF_TILE, D_MODEL), lambda i, j: (layer, j, 0)),
            pl.BlockSpec((1, D_MODEL), fixed),
            pl.BlockSpec((1, D_MODEL), fixed),
        ],
        out_specs=[pl.BlockSpec((ROW_TILE, D_MODEL), row), pl.BlockSpec((ROW_TILE, D_MODEL), row)],
        out_shape=[jax.ShapeDtypeStruct((L_PAD, D_MODEL), F32),
                   jax.ShapeDtypeStruct((L_PAD, D_MODEL), BF16)],
        scratch_shapes=[
            pltpu.VMEM((n_slots, ROW_TILE + CONV_HALO, FF_SUB), F32),
            pltpu.VMEM((N_FF_TILES, n_slots, CONV_HALO, FF_SUB), F32),
        ],
        compiler_params=_cparams("arbitrary", "arbitrary"),
        name="conv_ffn_ln",
    )(hb, hf, w_up, w_up, conv_w, conv_b, w_down, g, b)


def _gla_tables():
    c = GLA_CHUNK
    i = np.arange(c)[:, None]
    t = np.arange(c)[None, :]
    sums = [t <= i, t > i]
    masks = [i == t]
    s = c // 2
    while s >= 1:
        blk = i // s
        boundary = blk * s
        sums.append((blk % 2 == 1) & (t > boundary) & (t <= i))
        sums.append((blk % 2 == 0) & (t > i) & (t <= boundary + s))
        masks.append((blk % 2 == 1) & (t // s == blk - 1))
        s //= 2
    sums = np.concatenate(sums, axis=0).astype(np.float32)
    masks = np.concatenate(masks, axis=0).astype(np.float32)
    return np.concatenate([sums, sums, sums], axis=1), masks


def _gla_kernel(q_ref, k_ref, v_ref, r_ref, gl_ref, w2_ref, b2_ref, ng_ref, sum_ref, mask_ref,
                o_ref, st_ref):
    c = GLA_CHUNK
    n_levels = mask_ref.shape[0] // c - 1
    nt = (((1,), (1,)), ((), ()))

    @pl.when(pl.program_id(0) == 0)
    def _():
        st_ref[...] = jnp.zeros_like(st_ref)

    z = jnp.dot(gl_ref[...].astype(BF16), w2_ref[...], preferred_element_type=F32) + b2_ref[...]
    log2_a = _log_sigmoid(z) * (LOG2_E / GLA_TAU)
    sum_table = sum_ref[...]
    heads = range(GLA_HEADS)
    kc = [slice(h * GLA_HK, (h + 1) * GLA_HK) for h in heads]
    vc = [slice(h * GLA_HV, (h + 1) * GLA_HV) for h in heads]
    decay = [jnp.exp2(jnp.dot(sum_table, jnp.concatenate(_split3(log2_a[:, kc[h]]), axis=0),
                              preferred_element_type=F32)) for h in heads]
    qf = [q_ref[:, kc[h]].astype(F32) for h in heads]
    kf = [k_ref[:, kc[h]].astype(F32) for h in heads]
    o = [lax.dot_general((qf[h] * decay[h][0:c]).astype(BF16), st_ref[h].astype(BF16), nt,
                         preferred_element_type=F32) for h in heads]
    att = [mask_ref[0:c, :] * lax.dot_general(q_ref[:, kc[h]], k_ref[:, kc[h]], nt,
                                              preferred_element_type=F32) for h in heads]
    for lv in range(n_levels):
        r0 = (2 + 2 * lv) * c
        for h in heads:
            q_lv = (qf[h] * decay[h][r0:r0 + c]).astype(BF16)
            k_lv = (kf[h] * decay[h][r0 + c:r0 + 2 * c]).astype(BF16)
            att[h] = att[h] + mask_ref[(lv + 1) * c:(lv + 2) * c, :] * lax.dot_general(
                q_lv, k_lv, nt, preferred_element_type=F32)
    for h in heads:
        o[h] = o[h] + jnp.dot(att[h].astype(BF16), v_ref[:, vc[h]], preferred_element_type=F32)
    for h in heads:
        st_ref[h] = st_ref[h] * decay[h][c - 1:c, :] + lax.dot_general(
            v_ref[:, vc[h]], (kf[h] * decay[h][c:2 * c]).astype(BF16), (((0,), (0,)), ((), ())),
            preferred_element_type=F32)
    for h in heads:
        y = o[h] * lax.rsqrt(jnp.mean(o[h] * o[h], axis=-1, keepdims=True) + LN_EPS) * ng_ref[...]
        rg = r_ref[:, vc[h]].astype(F32)
        o_ref[:, vc[h]] = (y * (rg * jax.nn.sigmoid(rg))).astype(BF16)


def _gla_core(qkv, r, g_low, w_g2, b_g2, norm_g):
    c = GLA_CHUNK
    sums, masks = _gla_tables()
    fixed = lambda n: (0, 0)
    return pl.pallas_call(
        _gla_kernel,
        grid=(L_PAD // c,),
        in_specs=[
            pl.BlockSpec((c, GLA_DK), lambda n: (n, 0)),
            pl.BlockSpec((c, GLA_DK), lambda n: (n, 1)),
            pl.BlockSpec((c, GLA_DV), lambda n: (n, 1)),
            pl.BlockSpec((c, GLA_DV), lambda n: (n, 0)),
            pl.BlockSpec((c, LANE), lambda n: (n, 0)),
            pl.BlockSpec((LANE, GLA_DK), fixed),
            pl.BlockSpec((1, GLA_DK), fixed),
            pl.BlockSpec((1, GLA_HV), fixed),
            pl.BlockSpec(sums.shape, fixed),
            pl.BlockSpec(masks.shape, fixed),
        ],
        out_specs=pl.BlockSpec((c, GLA_DV), lambda n: (n, 0)),
        out_shape=jax.ShapeDtypeStruct((L_PAD, GLA_DV), BF16),
        scratch_shapes=[pltpu.VMEM((GLA_HEADS, GLA_HV, GLA_HK), F32)],
        compiler_params=_cparams("arbitrary"),
        name="gla_core",
    )(qkv, qkv, qkv, r, g_low, w_g2, b_g2, norm_g,
      jnp.asarray(sums, BF16), jnp.asarray(masks, F32))


def _cum_gate_kernel(fl_ref, bf_ref, o_ref, carry_ref):
    t = ROW_TILE

    @pl.when(pl.program_id(0) == 0)
    def _():
        carry_ref[...] = jnp.zeros_like(carry_ref)

    log_f = _log_sigmoid(fl_ref[...] + bf_ref[...]) * LOG2_E
    rid = lax.broadcasted_iota(jnp.int32, (t, t), 0)
    cid = lax.broadcasted_iota(jnp.int32, (t, t), 1)
    upper = jnp.where(rid <= cid, 1.0, 0.0).astype(BF16)
    cum = sum(lax.dot_general(p, upper, (((0,), (0,)), ((), ())), preferred_element_type=F32)
              for p in _split3(log_f))
    cum = cum + carry_ref[...]
    o_ref[...] = cum
    carry_ref[...] = cum[:, t - 1:t]


def _cum_gate(f_logit, bias):
    return pl.pallas_call(
        _cum_gate_kernel,
        grid=(N_ROW_TILES,),
        in_specs=[pl.BlockSpec((ROW_TILE, LANE), lambda i: (i, 0)),
                  pl.BlockSpec((1, LANE), lambda i: (0, 0))],
        out_specs=pl.BlockSpec((LANE, ROW_TILE), lambda i: (0, i)),
        out_shape=jax.ShapeDtypeStruct((LANE, L_PAD), F32),
        scratch_shapes=[pltpu.VMEM((LANE, 1), F32)],
        compiler_params=_cparams("arbitrary"),
        name="cum_forget_gate",
    )(f_logit, bias)


def _fox_kernel(q_ref, k_ref, v_ref, og_ref, c_ref, o_ref, m_sc, acc_sc):
    t = ROW_TILE
    qi = pl.program_id(1)
    rid = lax.broadcasted_iota(jnp.int32, (t, t), 0)
    cid = lax.broadcasted_iota(jnp.int32, (t, t), 1)
    causal = cid <= rid
    heads = range(FOX_GROUP)
    cols = [slice(g * FOX_HD, (g + 1) * FOX_HD) for g in heads]
    q_start = pl.multiple_of(qi * t, LANE)
    c_q0 = [c_ref[g, :, pl.ds(q_start, LANE)][:, 0:1] for g in heads]

    def score(g, start, w):
        return (lax.dot_general(q_ref[:, cols[g]], k_ref[pl.ds(start, w), cols[g]],
                                (((1,), (1,)), ((), ())), preferred_element_type=F32)
                + (c_q0[g] - c_ref[g, :, pl.ds(start, w)]))

    def softmax(g, s, mask):
        if mask:
            s = jnp.where(causal, s, NEG_BIG)
        m = m_sc[g]
        m_new = jnp.maximum(m, jnp.max(s, axis=-1, keepdims=True))
        m_sc[g] = m_new
        p = jnp.exp2(s - jnp.concatenate([m_new] * (s.shape[1] // LANE), axis=1)).astype(BF16)
        return p, jnp.exp2(m - m_new)

    def accumulate(g, start, p, alpha):
        w = p.shape[1]
        v_aug = jnp.concatenate([v_ref[pl.ds(start, w), cols[g]], jnp.ones((w, FOX_HD), BF16)],
                                axis=1)
        acc_sc[g] = jnp.concatenate([alpha, alpha], axis=1) * acc_sc[g] + jnp.dot(
            p, v_aug, preferred_element_type=F32)

    def block(start, w, mask):
        start = pl.multiple_of(start, LANE)
        s = [score(g, start, w) for g in heads]
        pa = [softmax(g, s[g], mask) for g in heads]
        for g in heads:
            accumulate(g, start, *pa[g])

    for g in heads:
        m_sc[g] = jnp.full((t, LANE), NEG_BIG, F32)
        acc_sc[g] = jnp.zeros((t, 2 * FOX_HD), F32)

    wide = FOX_KEY_TILES * t

    def wide_body(kb, _):
        block(kb * wide, wide, False)
        return 0

    def narrow_body(kb, _):
        block(kb * t, t, False)
        return 0

    n_wide = qi // FOX_KEY_TILES
    lax.fori_loop(0, n_wide, wide_body, 0)
    lax.fori_loop(n_wide * FOX_KEY_TILES, qi, narrow_body, 0)
    block(qi * t, t, True)
    for g in heads:
        acc = acc_sc[g]
        og = og_ref[:, cols[g]].astype(F32)
        o_ref[:, cols[g]] = (acc[:, :FOX_HD] / acc[:, FOX_HD:] * jax.nn.sigmoid(og)).astype(BF16)


def _fox_attention(q_og, kv, c_t):
    w = FOX_GROUP * FOX_HD
    n_groups = FOX_HEADS // FOX_GROUP
    return pl.pallas_call(
        _fox_kernel,
        grid=(n_groups, N_ROW_TILES),
        in_specs=[
            pl.BlockSpec((ROW_TILE, w), lambda hg, qi: (qi, hg)),
            pl.BlockSpec((L_PAD, w), lambda hg, qi: (0, hg), pipeline_mode=pl.Buffered(1)),
            pl.BlockSpec((L_PAD, w), lambda hg, qi: (0, n_groups + hg),
                         pipeline_mode=pl.Buffered(1)),
            pl.BlockSpec((ROW_TILE, w), lambda hg, qi: (qi, n_groups + hg)),
            pl.BlockSpec((FOX_GROUP, 1, L_PAD), lambda hg, qi: (hg, 0, 0)),
        ],
        out_specs=pl.BlockSpec((ROW_TILE, w), lambda hg, qi: (qi, hg)),
        out_shape=jax.ShapeDtypeStruct((L_PAD, D_MODEL), BF16),
        scratch_shapes=[pltpu.VMEM((FOX_GROUP, ROW_TILE, LANE), F32),
                        pltpu.VMEM((FOX_GROUP, ROW_TILE, 2 * FOX_HD), F32)],
        compiler_params=_cparams("parallel", "parallel"),
        name="fox_attention",
    )(q_og, kv, kv, q_og, c_t)


def _pad_cols(w, n):
    return jnp.pad(w, ((0, 0), (0, n - w.shape[1])))


def kernel(x, meta, ln_g, ln_b, gla_w_in, gla_w_g2, gla_b_g2, gla_norm_g, gla_w_out, kv_w, kv_bf,
           fox_w_in, fox_w_out, ffn_w_up, ffn_conv_w, ffn_conv_b, ffn_w_down):
    assert x.shape == (1, SEQ, D_MODEL)
    tail = jnp.zeros((L_PAD - L_VALID, D_MODEL), F32)
    hf = jnp.concatenate([meta.astype(F32), x[0], tail], axis=0)
    hb = hf.astype(BF16)

    ones = lambda n: jnp.ones((1, n), F32)
    row = lambda v: v.reshape(1, -1)
    g_lo, g_hi = 2 * GLA_DK + GLA_DV, 2 * GLA_DK + GLA_DV + GLA_RANK

    gla_in = gla_w_in[:, :, :g_lo].astype(BF16)
    gla_r = gla_w_in[:, :, g_hi:].astype(BF16)
    gla_g1 = jnp.pad(gla_w_in[:, :, g_lo:g_hi],
                     ((0, 0), (0, 0), (0, LANE - GLA_RANK))).astype(BF16)
    gla_g2 = jnp.pad(gla_w_g2, ((0, 0), (0, LANE - GLA_RANK), (0, 0))).astype(BF16)
    gla_out = gla_w_out.astype(BF16)
    kv_in = kv_w[:, :2 * D_MODEL].astype(BF16)[None]
    kv_f = _pad_cols(kv_w[:, 2 * D_MODEL:], LANE).astype(BF16)[None]
    fox_in = fox_w_in.astype(BF16)
    fox_out = fox_w_out.astype(BF16)
    ffn_up = ffn_w_up.astype(BF16)
    ffn_down = ffn_w_down.astype(BF16)
    ffn_cb = ffn_conv_b[:, None, :]
    gla_q_scale = jnp.concatenate([jnp.full((1, GLA_DK), GLA_HK ** -0.5, F32), ones(g_lo - GLA_DK)],
                                  axis=1)
    fox_q_scale = jnp.concatenate([jnp.full((1, D_MODEL), FOX_HD ** -0.5 * LOG2_E, F32),
                                   ones(D_MODEL)], axis=1)

    c_t = kv = None
    for l in range(DEPTH):
        if l < N_A_LAYERS:
            qkv = _matmul(hb, gla_in, l, g_lo, gla_q_scale, BF16, PROJ_TILE)
            r = _matmul(hb, gla_r, l, GLA_DV, ones(GLA_DV), BF16, PROJ_TILE)
            g_low = _matmul(hb, gla_g1, l, LANE, ones(LANE), F32, LANE)
            mix = _gla_core(qkv, r, g_low, gla_g2[l], row(gla_b_g2[l]), row(gla_norm_g[l]))
            w_out, j = gla_out, l
        else:
            j = l - N_A_LAYERS
            if j == 0:
                kv = _matmul(hb, kv_in, 0, 2 * D_MODEL, ones(2 * D_MODEL), BF16, PROJ_TILE)
                f_logit = _matmul(hb, kv_f, 0, LANE, ones(LANE), F32, LANE)
                c_t = _cum_gate(f_logit, _pad_cols(row(kv_bf), LANE))[:FOX_HEADS, None, :]
            q_og = _matmul(hb, fox_in, j, 2 * D_MODEL, fox_q_scale, BF16, PROJ_TILE)
            mix = _fox_attention(q_og, kv, c_t)
            w_out = fox_out
        hf, hb = _proj_residual_ln(mix, w_out, j, hf, row(ln_g[l, 0]), row(ln_b[l, 0]))
        hf, hb = _conv_ffn_ln(hb, hf, ffn_up, ffn_conv_w, ffn_cb, ffn_down, l,
                              row(ln_g[l, 1]), row(ln_b[l, 1]))
    return hf[N_META:L_VALID][None]
```

```python
import jax
import jax.numpy as jnp
import numpy as np
from jax import lax
from jax.experimental import pallas as pl
from jax.experimental.pallas import tpu as pltpu

F32 = jnp.float32
BF16 = jnp.bfloat16

D_MODEL = 2048
SEQ = 8192
DEPTH = 4
N_META = 16
N_A_LAYERS = DEPTH // 2
ALPHA = (2.0 * DEPTH) ** 0.25
LN_EPS = 1e-5

GLA_HEADS = 4
GLA_DK = D_MODEL // 2
GLA_DV = D_MODEL
GLA_HK = GLA_DK // GLA_HEADS
GLA_HV = GLA_DV // GLA_HEADS
GLA_RANK = 16
GLA_TAU = 16.0

FOX_HEADS = 16
FOX_HD = D_MODEL // FOX_HEADS

D_FF = 5632

LANE = 128
L_VALID = N_META + SEQ
ROW_TILE = 640
L_PAD = 13 * ROW_TILE
N_ROW_TILES = L_PAD // ROW_TILE

GLA_CHUNK = 64
PROJ_TILE = 1024
PROJ_ROWS = 13 * LANE
FF_TILE = 512
FF_SUB = 256
OUT_ROW_PIECES = 2
FF_ROW_PIECES = 2
N_FF_TILES = D_FF // FF_TILE
FOX_GROUP = 4
FOX_KEY_TILES = 2
FOX_SCORE_LEAD = 2
CONV_HALO = 8

VMEM_LIMIT = 56 * 1024 * 1024
NEG_BIG = -1e30
LOG2_E = 1.4426950408889634


def _cparams(*sem):
    return pltpu.CompilerParams(dimension_semantics=sem, vmem_limit_bytes=VMEM_LIMIT)


def _layer_norm(y, g, b):
    mu = jnp.mean(y, axis=-1, keepdims=True)
    d = y - mu
    var = jnp.mean(d * d, axis=-1, keepdims=True)
    return d * lax.rsqrt(var + LN_EPS) * g + b


def _log_sigmoid(z):
    return jnp.minimum(z, 0.0) - jnp.log1p(jnp.exp(-jnp.abs(z)))


def _split3(x):
    hi = x.astype(BF16)
    r1 = x - hi.astype(F32)
    mid = r1.astype(BF16)
    lo = (r1 - mid.astype(F32)).astype(BF16)
    return hi, mid, lo


def _mm_kernel(x_ref, w_ref, s_ref, o_ref):
    tn = o_ref.shape[1]
    acc = jnp.dot(x_ref[...], w_ref[...], preferred_element_type=F32)
    scale = s_ref[:, pl.ds(pl.multiple_of(pl.program_id(1) * tn, LANE), tn)]
    o_ref[...] = (acc * scale).astype(o_ref.dtype)


def _matmul(x, w, layer, n, col_scale, out_dtype, tn):
    k = x.shape[1]
    return pl.pallas_call(
        _mm_kernel,
        grid=(L_PAD // PROJ_ROWS, n // tn),
        in_specs=[
            pl.BlockSpec((PROJ_ROWS, k), lambda i, j: (i, 0)),
            pl.BlockSpec((None, k, tn), lambda i, j: (layer, 0, j)),
            pl.BlockSpec((1, n), lambda i, j: (0, 0)),
        ],
        out_specs=pl.BlockSpec((PROJ_ROWS, tn), lambda i, j: (i, j)),
        out_shape=jax.ShapeDtypeStruct((L_PAD, n), out_dtype),
        compiler_params=_cparams("parallel", "parallel"),
        name="proj_matmul",
    )(x, w, col_scale)


def _proj_ln_kernel(x_ref, w_ref, h_ref, g_ref, b_ref, of_ref, ob_ref):
    n = OUT_ROW_PIECES
    piece = ROW_TILE // n
    rows = [slice(r * piece, (r + 1) * piece) for r in range(n)]
    mm = lambda r: jnp.dot(x_ref[rows[r], :], w_ref[...], preferred_element_type=F32)
    acc = mm(0)
    for r in range(n):
        nxt = mm(r + 1) if r + 1 < n else None
        out = _layer_norm(ALPHA * h_ref[rows[r], :] + acc, g_ref[...], b_ref[...])
        of_ref[rows[r], :] = out
        ob_ref[rows[r], :] = out.astype(BF16)
        acc = nxt


def _proj_residual_ln(x, w, layer, h, g, b):
    row = lambda i: (i, 0)
    fixed = lambda i: (0, 0)
    return pl.pallas_call(
        _proj_ln_kernel,
        grid=(N_ROW_TILES,),
        in_specs=[
            pl.BlockSpec((ROW_TILE, D_MODEL), row),
            pl.BlockSpec((None, D_MODEL, D_MODEL), lambda i: (layer, 0, 0)),
            pl.BlockSpec((ROW_TILE, D_MODEL), row),
            pl.BlockSpec((1, D_MODEL), fixed),
            pl.BlockSpec((1, D_MODEL), fixed),
        ],
        out_specs=[pl.BlockSpec((ROW_TILE, D_MODEL), row), pl.BlockSpec((ROW_TILE, D_MODEL), row)],
        out_shape=[jax.ShapeDtypeStruct((L_PAD, D_MODEL), F32),
                   jax.ShapeDtypeStruct((L_PAD, D_MODEL), BF16)],
        compiler_params=_cparams("parallel"),
        name="out_proj_ln",
    )(x, w, h, g, b)


def _ffn_kernel(xb_ref, xf_ref, *refs):
    n_chunks = FF_TILE // FF_SUB
    wu_refs, wg_refs = refs[:n_chunks], refs[n_chunks:2 * n_chunks]
    cw_ref, cb_ref = refs[2 * n_chunks:2 * n_chunks + 2]
    wd_refs = refs[2 * n_chunks + 2:3 * n_chunks + 2]
    g_ref, b_ref, of_ref, ob_ref, ext_ref, carry_ref = refs[3 * n_chunks + 2:]
    i = pl.program_id(0)
    j = pl.program_id(1)

    @pl.when(i == 0)
    def _():
        carry_ref[j] = jnp.zeros(carry_ref.shape[1:], F32)

    @pl.when(j == 0)
    def _():
        of_ref[...] = jnp.zeros_like(of_ref)

    piece = ROW_TILE // FF_ROW_PIECES
    pieces = [slice(r * piece, (r + 1) * piece) for r in range(FF_ROW_PIECES)]

    def up(c, r):
        rows = pieces[r]
        for slot, w_ref in ((2 * c, wu_refs[c]), (2 * c + 1, wg_refs[c])):
            hval = jnp.dot(xb_ref[rows, :], w_ref[...], preferred_element_type=F32)
            if r == 0:
                ext_ref[slot, 0:CONV_HALO, :] = carry_ref[j, slot]
            if r == FF_ROW_PIECES - 1:
                carry_ref[j, slot] = hval[piece - CONV_HALO:, :]
            ext_ref[slot, CONV_HALO + rows.start:CONV_HALO + rows.stop, :] = hval

    def gate(c, r):
        def conv(slot, first_col):
            cols = pl.ds(pl.multiple_of(first_col, LANE), FF_SUB)
            cw = cw_ref[:, cols]
            lo = CONV_HALO + pieces[r].start
            return (cw[2:3, :] * ext_ref[slot, lo:lo + piece, :]
                    + cw[1:2, :] * ext_ref[slot, lo - 1:lo - 1 + piece, :]
                    + cw[0:1, :] * ext_ref[slot, lo - 2:lo - 2 + piece, :]
                    + cb_ref[:, cols])
        u = conv(2 * c, j * FF_TILE + c * FF_SUB)
        gt = conv(2 * c + 1, D_FF + j * FF_TILE + c * FF_SUB)
        return (gt * jax.nn.sigmoid(gt) * u).astype(BF16)

    for r in range(FF_ROW_PIECES):
        up(0, r)
    acts = [[None] * FF_ROW_PIECES for _ in range(n_chunks)]
    for c in range(n_chunks):
        for r in range(FF_ROW_PIECES):
            acts[c][r] = gate(c, r)
            if c + 1 < n_chunks:
                up(c + 1, r)
            else:
                rows = pieces[r]
                part = None
                for cc in range(n_chunks):
                    d = jnp.dot(acts[cc][r], wd_refs[cc][...], preferred_element_type=F32)
                    part = d if part is None else part + d
                of_ref[rows, :] += part

    @pl.when(j == N_FF_TILES - 1)
    def _():
        out = _layer_norm(ALPHA * xf_ref[...] + of_ref[...], g_ref[...], b_ref[...])
        of_ref[...] = out
        ob_ref[...] = out.astype(BF16)


def _conv_ffn_ln(hb, hf, w_up, conv_w, conv_b, w_down, layer, g, b):
    row = lambda i, j: (i, 0)
    fixed = lambda i, j: (0, 0)
    n_chunks = FF_TILE // FF_SUB
    n_slots = 2 * n_chunks
    gate0 = D_FF // FF_SUB

    def up_spec(first, c):
        return pl.BlockSpec((None, D_MODEL, FF_SUB), lambda i, j: (layer, 0, first + n_chunks * j + c))

    def down_spec(c):
        return pl.BlockSpec((None, FF_SUB, D_MODEL), lambda i, j: (layer, n_chunks * j + c, 0))

    return pl.pallas_call(
        _ffn_kernel,
        grid=(N_ROW_TILES, N_FF_TILES),
        in_specs=[
            pl.BlockSpec((ROW_TILE, D_MODEL), row, pipeline_mode=pl.Buffered(1)),
            pl.BlockSpec((ROW_TILE, D_MODEL), row, pipeline_mode=pl.Buffered(1)),
            *[up_spec(0, c) for c in range(n_chunks)],
            *[up_spec(gate0, c) for c in range(n_chunks)],
            pl.BlockSpec((None, 3, 2 * D_FF), lambda i, j: (layer, 0, 0)),
            pl.BlockSpec((None, 1, 2 * D_FF), lambda i, j: (layer, 0, 0)),
            *[down_spec(c) for c in range(n_chunks)],
            pl.BlockSpec((1, D_MODEL), fixed),
            pl.BlockSpec((1, D_MODEL), fixed),
        ],
        out_specs=[pl.BlockSpec((ROW_TILE, D_MODEL), row), pl.BlockSpec((ROW_TILE, D_MODEL), row)],
        out_shape=[jax.ShapeDtypeStruct((L_PAD, D_MODEL), F32),
                   jax.ShapeDtypeStruct((L_PAD, D_MODEL), BF16)],
        scratch_shapes=[
            pltpu.VMEM((n_slots, ROW_TILE + CONV_HALO, FF_SUB), F32),
            pltpu.VMEM((N_FF_TILES, n_slots, CONV_HALO, FF_SUB), F32),
        ],
        compiler_params=_cparams("arbitrary", "arbitrary"),
        name="conv_ffn_ln",
    )(hb, hf, *[w_up] * (2 * n_chunks), conv_w, conv_b, *[w_down] * n_chunks, g, b)


def _gla_tables():
    c = GLA_CHUNK
    i = np.arange(c)[:, None]
    t = np.arange(c)[None, :]
    sums = [t <= i, t > i]
    masks = [i == t]
    s = c // 2
    while s >= 1:
        blk = i // s
        boundary = blk * s
        sums.append((blk % 2 == 1) & (t > boundary) & (t <= i))
        sums.append((blk % 2 == 0) & (t > i) & (t <= boundary + s))
        masks.append((blk % 2 == 1) & (t // s == blk - 1))
        s //= 2
    sums = np.concatenate(sums, axis=0).astype(np.float32)
    masks = np.concatenate(masks, axis=0).astype(np.float32)
    return np.concatenate([sums, sums, sums], axis=1), masks


def _gla_kernel(q_ref, k_ref, v_ref, r_ref, gl_ref, w2_ref, b2_ref, ng_ref, sum_ref, mask_ref,
                o_ref, st_ref):
    c = GLA_CHUNK
    n_levels = mask_ref.shape[0] // c - 1
    nt = (((1,), (1,)), ((), ()))

    @pl.when(pl.program_id(0) == 0)
    def _():
        st_ref[...] = jnp.zeros_like(st_ref)

    z = jnp.dot(gl_ref[...].astype(BF16), w2_ref[...], preferred_element_type=F32) + b2_ref[...]
    log2_a = _log_sigmoid(z) * (LOG2_E / GLA_TAU)
    sum_table = sum_ref[...]
    heads = range(GLA_HEADS)
    kc = [slice(h * GLA_HK, (h + 1) * GLA_HK) for h in heads]
    vc = [slice(h * GLA_HV, (h + 1) * GLA_HV) for h in heads]
    decay = [jnp.exp2(jnp.dot(sum_table, jnp.concatenate(_split3(log2_a[:, kc[h]]), axis=0),
                              preferred_element_type=F32)) for h in heads]
    qf = [q_ref[:, kc[h]].astype(F32) for h in heads]
    kf = [k_ref[:, kc[h]].astype(F32) for h in heads]
    o = [lax.dot_general((qf[h] * decay[h][0:c]).astype(BF16), st_ref[h].astype(BF16), nt,
                         preferred_element_type=F32) for h in heads]
    att = [mask_ref[0:c, :] * lax.dot_general(q_ref[:, kc[h]], k_ref[:, kc[h]], nt,
                                              preferred_element_type=F32) for h in heads]
    for lv in range(n_levels):
        r0 = (2 + 2 * lv) * c
        for h in heads:
            q_lv = (qf[h] * decay[h][r0:r0 + c]).astype(BF16)
            k_lv = (kf[h] * decay[h][r0 + c:r0 + 2 * c]).astype(BF16)
            att[h] = att[h] + mask_ref[(lv + 1) * c:(lv + 2) * c, :] * lax.dot_general(
                q_lv, k_lv, nt, preferred_element_type=F32)
    for h in heads:
        o[h] = o[h] + jnp.dot(att[h].astype(BF16), v_ref[:, vc[h]], preferred_element_type=F32)
    for h in heads:
        st_ref[h] = st_ref[h] * decay[h][c - 1:c, :] + lax.dot_general(
            v_ref[:, vc[h]], (kf[h] * decay[h][c:2 * c]).astype(BF16), (((0,), (0,)), ((), ())),
            preferred_element_type=F32)
    for h in heads:
        y = o[h] * lax.rsqrt(jnp.mean(o[h] * o[h], axis=-1, keepdims=True) + LN_EPS) * ng_ref[...]
        rg = r_ref[:, vc[h]].astype(F32)
        o_ref[:, vc[h]] = (y * (rg * jax.nn.sigmoid(rg))).astype(BF16)


def _gla_core(qkv, r, g_low, w_g2, b_g2, norm_g):
    c = GLA_CHUNK
    sums, masks = _gla_tables()
    fixed = lambda n: (0, 0)
    return pl.pallas_call(
        _gla_kernel,
        grid=(L_PAD // c,),
        in_specs=[
            pl.BlockSpec((c, GLA_DK), lambda n: (n, 0)),
            pl.BlockSpec((c, GLA_DK), lambda n: (n, 1)),
            pl.BlockSpec((c, GLA_DV), lambda n: (n, 1)),
            pl.BlockSpec((c, GLA_DV), lambda n: (n, 0)),
            pl.BlockSpec((c, LANE), lambda n: (n, 0)),
            pl.BlockSpec((LANE, GLA_DK), fixed),
            pl.BlockSpec((1, GLA_DK), fixed),
            pl.BlockSpec((1, GLA_HV), fixed),
            pl.BlockSpec(sums.shape, fixed),
            pl.BlockSpec(masks.shape, fixed),
        ],
        out_specs=pl.BlockSpec((c, GLA_DV), lambda n: (n, 0)),
        out_shape=jax.ShapeDtypeStruct((L_PAD, GLA_DV), BF16),
        scratch_shapes=[pltpu.VMEM((GLA_HEADS, GLA_HV, GLA_HK), F32)],
        compiler_params=_cparams("arbitrary"),
        name="gla_core",
    )(qkv, qkv, qkv, r, g_low, w_g2, b_g2, norm_g,
      jnp.asarray(sums, BF16), jnp.asarray(masks, F32))


def _cum_gate_kernel(fl_ref, bf_ref, o_ref, carry_ref):
    t = ROW_TILE

    @pl.when(pl.program_id(0) == 0)
    def _():
        carry_ref[...] = jnp.zeros_like(carry_ref)

    log_f = _log_sigmoid(fl_ref[...] + bf_ref[...]) * LOG2_E
    rid = lax.broadcasted_iota(jnp.int32, (t, t), 0)
    cid = lax.broadcasted_iota(jnp.int32, (t, t), 1)
    upper = jnp.where(rid <= cid, 1.0, 0.0).astype(BF16)
    cum = sum(lax.dot_general(p, upper, (((0,), (0,)), ((), ())), preferred_element_type=F32)
              for p in _split3(log_f))
    cum = cum + carry_ref[...]
    o_ref[...] = cum
    carry_ref[...] = cum[:, t - 1:t]


def _cum_gate(f_logit, bias):
    return pl.pallas_call(
        _cum_gate_kernel,
        grid=(N_ROW_TILES,),
        in_specs=[pl.BlockSpec((ROW_TILE, LANE), lambda i: (i, 0)),
                  pl.BlockSpec((1, LANE), lambda i: (0, 0))],
        out_specs=pl.BlockSpec((LANE, ROW_TILE), lambda i: (0, i)),
        out_shape=jax.ShapeDtypeStruct((LANE, L_PAD), F32),
        scratch_shapes=[pltpu.VMEM((LANE, 1), F32)],
        compiler_params=_cparams("arbitrary"),
        name="cum_forget_gate",
    )(f_logit, bias)


def _fox_kernel(q_ref, k_ref, v_ref, og_ref, c_ref, o_ref, m_sc, acc_sc):
    t = ROW_TILE
    qi = pl.program_id(1)
    rid = lax.broadcasted_iota(jnp.int32, (t, t), 0)
    cid = lax.broadcasted_iota(jnp.int32, (t, t), 1)
    causal = cid <= rid
    heads = range(FOX_GROUP)
    cols = [slice(g * FOX_HD, (g + 1) * FOX_HD) for g in heads]
    q_start = pl.multiple_of(qi * t, LANE)
    c_q0 = [c_ref[g, :, pl.ds(q_start, LANE)][:, 0:1] for g in heads]

    half = t // 2
    units = [(g, slice(r * half, (r + 1) * half)) for g in heads for r in range(2)]

    def score(unit, start, w):
        g, rows = unit
        return (lax.dot_general(q_ref[rows, cols[g]], k_ref[pl.ds(start, w), cols[g]],
                                (((1,), (1,)), ((), ())), preferred_element_type=F32)
                + (c_q0[g] - c_ref[g, :, pl.ds(start, w)]))

    def softmax(unit, s, mask):
        g, rows = unit
        if mask:
            s = jnp.where(causal[rows, :], s, NEG_BIG)
        m = m_sc[g, rows, :]
        m_new = jnp.maximum(m, jnp.max(s, axis=-1, keepdims=True))
        m_sc[g, rows, :] = m_new
        p = jnp.exp2(s - jnp.concatenate([m_new] * (s.shape[1] // LANE), axis=1)).astype(BF16)
        return p, jnp.exp2(m - m_new)

    def accumulate(unit, start, p, alpha):
        g, rows = unit
        w = p.shape[1]
        v_aug = jnp.concatenate([v_ref[pl.ds(start, w), cols[g]], jnp.ones((w, FOX_HD), BF16)],
                                axis=1)
        acc_sc[g, rows, :] = jnp.concatenate([alpha, alpha], axis=1) * acc_sc[g, rows, :] + jnp.dot(
            p, v_aug, preferred_element_type=F32)

    def block(start, w, mask):
        start = pl.multiple_of(start, LANE)
        n = len(units)
        lead = FOX_SCORE_LEAD
        s = {k: score(units[k], start, w) for k in range(lead)}
        pa = {}
        for k in range(n):
            pa[k] = softmax(units[k], s.pop(k), mask)
            if k + lead < n:
                s[k + lead] = score(units[k + lead], start, w)
            if k >= 1:
                accumulate(units[k - 1], start, *pa.pop(k - 1))
        accumulate(units[n - 1], start, *pa.pop(n - 1))

    for g in heads:
        m_sc[g] = jnp.full((t, LANE), NEG_BIG, F32)
        acc_sc[g] = jnp.zeros((t, 2 * FOX_HD), F32)

    wide = FOX_KEY_TILES * t

    def wide_body(kb, _):
        block(kb * wide, wide, False)
        return 0

    def narrow_body(kb, _):
        block(kb * t, t, False)
        return 0

    n_wide = qi // FOX_KEY_TILES
    lax.fori_loop(0, n_wide, wide_body, 0)
    lax.fori_loop(n_wide * FOX_KEY_TILES, qi, narrow_body, 0)
    block(qi * t, t, True)
    for g in heads:
        acc = acc_sc[g]
        og = og_ref[:, cols[g]].astype(F32)
        o_ref[:, cols[g]] = (acc[:, :FOX_HD] / acc[:, FOX_HD:] * jax.nn.sigmoid(og)).astype(BF16)


def _fox_attention(q_og, kv, c_t):
    w = FOX_GROUP * FOX_HD
    n_groups = FOX_HEADS // FOX_GROUP
    return pl.pallas_call(
        _fox_kernel,
        grid=(n_groups, N_ROW_TILES),
        in_specs=[
            pl.BlockSpec((ROW_TILE, w), lambda hg, qi: (qi, hg)),
            pl.BlockSpec((L_PAD, w), lambda hg, qi: (0, hg), pipeline_mode=pl.Buffered(1)),
            pl.BlockSpec((L_PAD, w), lambda hg, qi: (0, n_groups + hg),
                         pipeline_mode=pl.Buffered(1)),
            pl.BlockSpec((ROW_TILE, w), lambda hg, qi: (qi, n_groups + hg)),
            pl.BlockSpec((FOX_GROUP, 1, L_PAD), lambda hg, qi: (hg, 0, 0)),
        ],
        out_specs=pl.BlockSpec((ROW_TILE, w), lambda hg, qi: (qi, hg)),
        out_shape=jax.ShapeDtypeStruct((L_PAD, D_MODEL), BF16),
        scratch_shapes=[pltpu.VMEM((FOX_GROUP, ROW_TILE, LANE), F32),
                        pltpu.VMEM((FOX_GROUP, ROW_TILE, 2 * FOX_HD), F32)],
        compiler_params=_cparams("parallel", "parallel"),
        name="fox_attention",
    )(q_og, kv, kv, q_og, c_t)


def _pad_cols(w, n):
    return jnp.pad(w, ((0, 0), (0, n - w.shape[1])))


def kernel(x, meta, ln_g, ln_b, gla_w_in, gla_w_g2, gla_b_g2, gla_norm_g, gla_w_out, kv_w, kv_bf,
           fox_w_in, fox_w_out, ffn_w_up, ffn_conv_w, ffn_conv_b, ffn_w_down):
    assert x.shape == (1, SEQ, D_MODEL)
    tail = jnp.zeros((L_PAD - L_VALID, D_MODEL), F32)
    hf = jnp.concatenate([meta.astype(F32), x[0], tail], axis=0)
    hb = hf.astype(BF16)

    ones = lambda n: jnp.ones((1, n), F32)
    row = lambda v: v.reshape(1, -1)
    g_lo, g_hi = 2 * GLA_DK + GLA_DV, 2 * GLA_DK + GLA_DV + GLA_RANK

    gla_in = gla_w_in[:, :, :g_lo].astype(BF16)
    gla_r = gla_w_in[:, :, g_hi:].astype(BF16)
    gla_g1 = jnp.pad(gla_w_in[:, :, g_lo:g_hi],
                     ((0, 0), (0, 0), (0, LANE - GLA_RANK))).astype(BF16)
    gla_g2 = jnp.pad(gla_w_g2, ((0, 0), (0, LANE - GLA_RANK), (0, 0))).astype(BF16)
    gla_out = gla_w_out.astype(BF16)
    kv_in = kv_w[:, :2 * D_MODEL].astype(BF16)[None]
    kv_f = _pad_cols(kv_w[:, 2 * D_MODEL:], LANE).astype(BF16)[None]
    fox_in = fox_w_in.astype(BF16)
    fox_out = fox_w_out.astype(BF16)
    ffn_up = ffn_w_up.astype(BF16)
    ffn_down = ffn_w_down.astype(BF16)
    ffn_cb = ffn_conv_b[:, None, :]
    gla_q_scale = jnp.concatenate([jnp.full((1, GLA_DK), GLA_HK ** -0.5, F32), ones(g_lo - GLA_DK)],
                                  axis=1)
    fox_q_scale = jnp.concatenate([jnp.full((1, D_MODEL), FOX_HD ** -0.5 * LOG2_E, F32),
                                   ones(D_MODEL)], axis=1)

    c_t = kv = None
    for l in range(DEPTH):
        if l < N_A_LAYERS:
            qkv = _matmul(hb, gla_in, l, g_lo, gla_q_scale, BF16, PROJ_TILE)
            r = _matmul(hb, gla_r, l, GLA_DV, ones(GLA_DV), BF16, PROJ_TILE)
            g_low = _matmul(hb, gla_g1, l, LANE, ones(LANE), F32, LANE)
            mix = _gla_core(qkv, r, g_low, gla_g2[l], row(gla_b_g2[l]), row(gla_norm_g[l]))
            w_out, j = gla_out, l
        else:
            j = l - N_A_LAYERS
            if j == 0:
                kv = _matmul(hb, kv_in, 0, 2 * D_MODEL, ones(2 * D_MODEL), BF16, PROJ_TILE)
                f_logit = _matmul(hb, kv_f, 0, LANE, ones(LANE), F32, LANE)
                c_t = _cum_gate(f_logit, _pad_cols(row(kv_bf), LANE))[:FOX_HEADS, None, :]
            q_og = _matmul(hb, fox_in, j, 2 * D_MODEL, fox_q_scale, BF16, PROJ_TILE)
            mix = _fox_attention(q_og, kv, c_t)
            w_out = fox_out
        hf, hb = _proj_residual_ln(mix, w_out, j, hf, row(ln_g[l, 0]), row(ln_b[l, 0]))
        hf, hb = _conv_ffn_ln(hb, hf, ffn_up, ffn_conv_w, ffn_cb, ffn_down, l,
                              row(ln_g[l, 1]), row(ln_b[l, 1]))
    return hf[N_META:L_VALID][None]
```

```python
import jax
import jax.numpy as jnp
import numpy as np
from jax import lax
from jax.experimental import pallas as pl
from jax.experimental.pallas import tpu as pltpu

F32 = jnp.float32
BF16 = jnp.bfloat16

D_MODEL = 2048
SEQ = 8192
DEPTH = 4
N_META = 16
N_A_LAYERS = DEPTH // 2
ALPHA = (2.0 * DEPTH) ** 0.25
LN_EPS = 1e-5

GLA_HEADS = 4
GLA_DK = D_MODEL // 2
GLA_DV = D_MODEL
GLA_HK = GLA_DK // GLA_HEADS
GLA_HV = GLA_DV // GLA_HEADS
GLA_RANK = 16
GLA_TAU = 16.0

FOX_HEADS = 16
FOX_HD = D_MODEL // FOX_HEADS

D_FF = 5632

LANE = 128
L_VALID = N_META + SEQ
ROW_TILE = 640
L_PAD = 13 * ROW_TILE
N_ROW_TILES = L_PAD // ROW_TILE

GLA_CHUNK = 64
PROJ_TILE = 1024
PROJ_ROWS = 13 * LANE
FF_TILE = 512
FF_SUB = 256
OUT_ROW_PIECES = 2
FF_ROW_PIECES = 2
N_FF_TILES = D_FF // FF_TILE
FOX_GROUP = 4
FOX_KEY_TILES = 2
FOX_SCORE_LEAD = 2
CONV_HALO = 8

VMEM_LIMIT = 56 * 1024 * 1024
NEG_BIG = -1e30
LOG2_E = 1.4426950408889634


def _cparams(*sem):
    return pltpu.CompilerParams(dimension_semantics=sem, vmem_limit_bytes=VMEM_LIMIT)


def _layer_norm(y, g, b):
    mu = jnp.mean(y, axis=-1, keepdims=True)
    d = y - mu
    var = jnp.mean(d * d, axis=-1, keepdims=True)
    return d * lax.rsqrt(var + LN_EPS) * g + b


def _log_sigmoid(z):
    return jnp.minimum(z, 0.0) - jnp.log1p(jnp.exp(-jnp.abs(z)))


def _split3(x):
    hi = x.astype(BF16)
    r1 = x - hi.astype(F32)
    mid = r1.astype(BF16)
    lo = (r1 - mid.astype(F32)).astype(BF16)
    return hi, mid, lo


def _mm_kernel(x_ref, w_ref, s_ref, o_ref):
    tn = o_ref.shape[1]
    acc = jnp.dot(x_ref[...], w_ref[...], preferred_element_type=F32)
    scale = s_ref[:, pl.ds(pl.multiple_of(pl.program_id(1) * tn, LANE), tn)]
    o_ref[...] = (acc * scale).astype(o_ref.dtype)


def _matmul(x, w, layer, n, col_scale, out_dtype, tn):
    k = x.shape[1]
    return pl.pallas_call(
        _mm_kernel,
        grid=(L_PAD // PROJ_ROWS, n // tn),
        in_specs=[
            pl.BlockSpec((PROJ_ROWS, k), lambda i, j: (i, 0)),
            pl.BlockSpec((None, k, tn), lambda i, j: (layer, 0, j)),
            pl.BlockSpec((1, n), lambda i, j: (0, 0)),
        ],
        out_specs=pl.BlockSpec((PROJ_ROWS, tn), lambda i, j: (i, j)),
        out_shape=jax.ShapeDtypeStruct((L_PAD, n), out_dtype),
        compiler_params=_cparams("parallel", "parallel"),
        name="proj_matmul",
    )(x, w, col_scale)


def _proj_ln_kernel(x_ref, w_ref, h_ref, g_ref, b_ref, of_ref, ob_ref):
    n = OUT_ROW_PIECES
    piece = ROW_TILE // n
    rows = [slice(r * piece, (r + 1) * piece) for r in range(n)]
    mm = lambda r: jnp.dot(x_ref[rows[r], :], w_ref[...], preferred_element_type=F32)
    acc = mm(0)
    for r in range(n):
        nxt = mm(r + 1) if r + 1 < n else None
        out = _layer_norm(ALPHA * h_ref[rows[r], :] + acc, g_ref[...], b_ref[...])
        of_ref[rows[r], :] = out
        ob_ref[rows[r], :] = out.astype(BF16)
        acc = nxt


def _proj_residual_ln(x, w, layer, h, g, b):
    row = lambda i: (i, 0)
    fixed = lambda i: (0, 0)
    return pl.pallas_call(
        _proj_ln_kernel,
        grid=(N_ROW_TILES,),
        in_specs=[
            pl.BlockSpec((ROW_TILE, D_MODEL), row),
            pl.BlockSpec((None, D_MODEL, D_MODEL), lambda i: (layer, 0, 0)),
            pl.BlockSpec((ROW_TILE, D_MODEL), row),
            pl.BlockSpec((1, D_MODEL), fixed),
            pl.BlockSpec((1, D_MODEL), fixed),
        ],
        out_specs=[pl.BlockSpec((ROW_TILE, D_MODEL), row), pl.BlockSpec((ROW_TILE, D_MODEL), row)],
        out_shape=[jax.ShapeDtypeStruct((L_PAD, D_MODEL), F32),
                   jax.ShapeDtypeStruct((L_PAD, D_MODEL), BF16)],
        compiler_params=_cparams("parallel"),
        name="out_proj_ln",
    )(x, w, h, g, b)


def _ffn_kernel(xb_ref, xf_ref, wu_ref, wg_ref, cw_ref, cb_ref, wd_ref,
                g_ref, b_ref, of_ref, ob_ref, ext_ref, carry_ref):
    i = pl.program_id(0)
    j = pl.program_id(1)

    @pl.when(i == 0)
    def _():
        carry_ref[j] = jnp.zeros(carry_ref.shape[1:], F32)

    @pl.when(j == 0)
    def _():
        of_ref[...] = jnp.zeros_like(of_ref)

    piece = ROW_TILE // FF_ROW_PIECES
    pieces = [slice(r * piece, (r + 1) * piece) for r in range(FF_ROW_PIECES)]
    n_chunks = FF_TILE // FF_SUB
    chunks = [slice(c * FF_SUB, (c + 1) * FF_SUB) for c in range(n_chunks)]

    def up(c, r):
        rows = pieces[r]
        for slot, w_ref in ((2 * c, wu_ref), (2 * c + 1, wg_ref)):
            hval = jnp.dot(xb_ref[rows, :], w_ref[:, chunks[c]], preferred_element_type=F32)
            if r == 0:
                ext_ref[slot, 0:CONV_HALO, :] = carry_ref[j, slot]
            if r == FF_ROW_PIECES - 1:
                carry_ref[j, slot] = hval[piece - CONV_HALO:, :]
            ext_ref[slot, CONV_HALO + rows.start:CONV_HALO + rows.stop, :] = hval

    def gate(c, r):
        def conv(slot, first_col):
            cols = pl.ds(pl.multiple_of(first_col, LANE), FF_SUB)
            cw = cw_ref[:, cols]
            lo = CONV_HALO + pieces[r].start
            return (cw[2:3, :] * ext_ref[slot, lo:lo + piece, :]
                    + cw[1:2, :] * ext_ref[slot, lo - 1:lo - 1 + piece, :]
                    + cw[0:1, :] * ext_ref[slot, lo - 2:lo - 2 + piece, :]
                    + cb_ref[:, cols])
        u = conv(2 * c, j * FF_TILE + c * FF_SUB)
        gt = conv(2 * c + 1, D_FF + j * FF_TILE + c * FF_SUB)
        return (gt * jax.nn.sigmoid(gt) * u).astype(BF16)

    for r in range(FF_ROW_PIECES):
        up(0, r)
    acts = [[None] * FF_ROW_PIECES for _ in range(n_chunks)]
    for c in range(n_chunks):
        for r in range(FF_ROW_PIECES):
            acts[c][r] = gate(c, r)
            if c + 1 < n_chunks:
                up(c + 1, r)
            else:
                rows = pieces[r]
                part = None
                for cc in range(n_chunks):
                    d = jnp.dot(acts[cc][r], wd_ref[chunks[cc], :], preferred_element_type=F32)
                    part = d if part is None else part + d
                of_ref[rows, :] += part

    @pl.when(j == N_FF_TILES - 1)
    def _():
        out = _layer_norm(ALPHA * xf_ref[...] + of_ref[...], g_ref[...], b_ref[...])
        of_ref[...] = out
        ob_ref[...] = out.astype(BF16)


def _conv_ffn_ln(hb, hf, w_up, conv_w, conv_b, w_down, layer, g, b):
    row = lambda i, j: (i, 0)
    fixed = lambda i, j: (0, 0)
    ucol = lambda i, j: (layer, 0, j)
    gcol = lambda i, j: (layer, 0, N_FF_TILES + j)
    n_slots = 2 * FF_TILE // FF_SUB
    return pl.pallas_call(
        _ffn_kernel,
        grid=(N_ROW_TILES, N_FF_TILES),
        in_specs=[
            pl.BlockSpec((ROW_TILE, D_MODEL), row, pipeline_mode=pl.Buffered(1)),
            pl.BlockSpec((ROW_TILE, D_MODEL), row, pipeline_mode=pl.Buffered(1)),
            pl.BlockSpec((None, D_MODEL, FF_TILE), ucol),
            pl.BlockSpec((None, D_MODEL, FF_TILE), gcol),
            pl.BlockSpec((None, 3, 2 * D_FF), lambda i, j: (layer, 0, 0)),
            pl.BlockSpec((None, 1, 2 * D_FF), lambda i, j: (layer, 0, 0)),
            pl.BlockSpec((None, FF_TILE, D_MODEL), lambda i, j: (layer, j, 0)),
            pl.BlockSpec((1, D_MODEL), fixed),
            pl.BlockSpec((1, D_MODEL), fixed),
        ],
        out_specs=[pl.BlockSpec((ROW_TILE, D_MODEL), row), pl.BlockSpec((ROW_TILE, D_MODEL), row)],
        out_shape=[jax.ShapeDtypeStruct((L_PAD, D_MODEL), F32),
                   jax.ShapeDtypeStruct((L_PAD, D_MODEL), BF16)],
        scratch_shapes=[
            pltpu.VMEM((n_slots, ROW_TILE + CONV_HALO, FF_SUB), F32),
            pltpu.VMEM((N_FF_TILES, n_slots, CONV_HALO, FF_SUB), F32),
        ],
        compiler_params=_cparams("arbitrary", "arbitrary"),
        name="conv_ffn_ln",
    )(hb, hf, w_up, w_up, conv_w, conv_b, w_down, g, b)


def _gla_tables():
    c = GLA_CHUNK
    i = np.arange(c)[:, None]
    t = np.arange(c)[None, :]
    sums = [t <= i, t > i]
    masks = [i == t]
    s = c // 2
    while s >= 1:
        blk = i // s
        boundary = blk * s
        sums.append((blk % 2 == 1) & (t > boundary) & (t <= i))
        sums.append((blk % 2 == 0) & (t > i) & (t <= boundary + s))
        masks.append((blk % 2 == 1) & (t // s == blk - 1))
        s //= 2
    sums = np.concatenate(sums, axis=0).astype(np.float32)
    masks = np.concatenate(masks, axis=0).astype(np.float32)
    return np.concatenate([sums, sums, sums], axis=1), masks


def _gla_kernel(q_ref, k_ref, v_ref, r_ref, gl_ref, w2_ref, b2_ref, ng_ref, sum_ref, mask_ref,
                o_ref, st_ref):
    c = GLA_CHUNK
    n_levels = mask_ref.shape[0] // c - 1
    nt = (((1,), (1,)), ((), ()))

    @pl.when(pl.program_id(0) == 0)
    def _():
        st_ref[...] = jnp.zeros_like(st_ref)

    z = jnp.dot(gl_ref[...].astype(BF16), w2_ref[...], preferred_element_type=F32) + b2_ref[...]
    log2_a = _log_sigmoid(z) * (LOG2_E / GLA_TAU)
    sum_table = sum_ref[...]
    heads = range(GLA_HEADS)
    kc = [slice(h * GLA_HK, (h + 1) * GLA_HK) for h in heads]
    vc = [slice(h * GLA_HV, (h + 1) * GLA_HV) for h in heads]
    decay = [jnp.exp2(jnp.dot(sum_table, jnp.concatenate(_split3(log2_a[:, kc[h]]), axis=0),
                              preferred_element_type=F32)) for h in heads]
    qf = [q_ref[:, kc[h]].astype(F32) for h in heads]
    kf = [k_ref[:, kc[h]].astype(F32) for h in heads]
    o = [lax.dot_general((qf[h] * decay[h][0:c]).astype(BF16), st_ref[h].astype(BF16), nt,
                         preferred_element_type=F32) for h in heads]
    att = [mask_ref[0:c, :] * lax.dot_general(q_ref[:, kc[h]], k_ref[:, kc[h]], nt,
                                              preferred_element_type=F32) for h in heads]
    for lv in range(n_levels):
        r0 = (2 + 2 * lv) * c
        for h in heads:
            q_lv = (qf[h] * decay[h][r0:r0 + c]).astype(BF16)
            k_lv = (kf[h] * decay[h][r0 + c:r0 + 2 * c]).astype(BF16)
            att[h] = att[h] + mask_ref[(lv + 1) * c:(lv + 2) * c, :] * lax.dot_general(
                q_lv, k_lv, nt, preferred_element_type=F32)
    for h in heads:
        o[h] = o[h] + jnp.dot(att[h].astype(BF16), v_ref[:, vc[h]], preferred_element_type=F32)
    for h in heads:
        st_ref[h] = st_ref[h] * decay[h][c - 1:c, :] + lax.dot_general(
            v_ref[:, vc[h]], (kf[h] * decay[h][c:2 * c]).astype(BF16), (((0,), (0,)), ((), ())),
            preferred_element_type=F32)
    for h in heads:
        y = o[h] * lax.rsqrt(jnp.mean(o[h] * o[h], axis=-1, keepdims=True) + LN_EPS) * ng_ref[...]
        rg = r_ref[:, vc[h]].astype(F32)
        o_ref[:, vc[h]] = (y * (rg * jax.nn.sigmoid(rg))).astype(BF16)


def _gla_core(qkv, r, g_low, w_g2, b_g2, norm_g):
    c = GLA_CHUNK
    sums, masks = _gla_tables()
    fixed = lambda n: (0, 0)
    return pl.pallas_call(
        _gla_kernel,
        grid=(L_PAD // c,),
        in_specs=[
            pl.BlockSpec((c, GLA_DK), lambda n: (n, 0)),
            pl.BlockSpec((c, GLA_DK), lambda n: (n, 1)),
            pl.BlockSpec((c, GLA_DV), lambda n: (n, 1)),
            pl.BlockSpec((c, GLA_DV), lambda n: (n, 0)),
            pl.BlockSpec((c, LANE), lambda n: (n, 0)),
            pl.BlockSpec((LANE, GLA_DK), fixed),
            pl.BlockSpec((1, GLA_DK), fixed),
            pl.BlockSpec((1, GLA_HV), fixed),
            pl.BlockSpec(sums.shape, fixed),
            pl.BlockSpec(masks.shape, fixed),
        ],
        out_specs=pl.BlockSpec((c, GLA_DV), lambda n: (n, 0)),
        out_shape=jax.ShapeDtypeStruct((L_PAD, GLA_DV), BF16),
        scratch_shapes=[pltpu.VMEM((GLA_HEADS, GLA_HV, GLA_HK), F32)],
        compiler_params=_cparams("arbitrary"),
        name="gla_core",
    )(qkv, qkv, qkv, r, g_low, w_g2, b_g2, norm_g,
      jnp.asarray(sums, BF16), jnp.asarray(masks, F32))


def _cum_gate_kernel(fl_ref, bf_ref, o_ref, carry_ref):
    t = ROW_TILE

    @pl.when(pl.program_id(0) == 0)
    def _():
        carry_ref[...] = jnp.zeros_like(carry_ref)

    log_f = _log_sigmoid(fl_ref[...] + bf_ref[...]) * LOG2_E
    rid = lax.broadcasted_iota(jnp.int32, (t, t), 0)
    cid = lax.broadcasted_iota(jnp.int32, (t, t), 1)
    upper = jnp.where(rid <= cid, 1.0, 0.0).astype(BF16)
    cum = sum(lax.dot_general(p, upper, (((0,), (0,)), ((), ())), preferred_element_type=F32)
              for p in _split3(log_f))
    cum = cum + carry_ref[...]
    o_ref[...] = cum
    carry_ref[...] = cum[:, t - 1:t]


def _cum_gate(f_logit, bias):
    return pl.pallas_call(
        _cum_gate_kernel,
        grid=(N_ROW_TILES,),
        in_specs=[pl.BlockSpec((ROW_TILE, LANE), lambda i: (i, 0)),
                  pl.BlockSpec((1, LANE), lambda i: (0, 0))],
        out_specs=pl.BlockSpec((LANE, ROW_TILE), lambda i: (0, i)),
        out_shape=jax.ShapeDtypeStruct((LANE, L_PAD), F32),
        scratch_shapes=[pltpu.VMEM((LANE, 1), F32)],
        compiler_params=_cparams("arbitrary"),
        name="cum_forget_gate",
    )(f_logit, bias)


def _fox_kernel(q_ref, k_ref, v_ref, og_ref, c_ref, o_ref, m_sc, acc_sc):
    t = ROW_TILE
    qi = pl.program_id(1)
    rid = lax.broadcasted_iota(jnp.int32, (t, t), 0)
    cid = lax.broadcasted_iota(jnp.int32, (t, t), 1)
    causal = cid <= rid
    heads = range(FOX_GROUP)
    cols = [slice(g * FOX_HD, (g + 1) * FOX_HD) for g in heads]
    q_start = pl.multiple_of(qi * t, LANE)
    c_q0 = [c_ref[g, :, pl.ds(q_start, LANE)][:, 0:1] for g in heads]

    half = t // 2
    units = [(g, slice(r * half, (r + 1) * half)) for g in heads for r in range(2)]

    def score(unit, start, w):
        g, rows = unit
        return (lax.dot_general(q_ref[rows, cols[g]], k_ref[pl.ds(start, w), cols[g]],
                                (((1,), (1,)), ((), ())), preferred_element_type=F32)
                + (c_q0[g] - c_ref[g, :, pl.ds(start, w)]))

    def softmax(unit, s, mask):
        g, rows = unit
        if mask:
            s = jnp.where(causal[rows, :], s, NEG_BIG)
        m = m_sc[g, rows, :]
        m_new = jnp.maximum(m, jnp.max(s, axis=-1, keepdims=True))
        m_sc[g, rows, :] = m_new
        p = jnp.exp2(s - jnp.concatenate([m_new] * (s.shape[1] // LANE), axis=1)).astype(BF16)
        return p, jnp.exp2(m - m_new)

    def accumulate(unit, start, p, alpha):
        g, rows = unit
        w = p.shape[1]
        v_aug = jnp.concatenate([v_ref[pl.ds(start, w), cols[g]], jnp.ones((w, FOX_HD), BF16)],
                                axis=1)
        acc_sc[g, rows, :] = jnp.concatenate([alpha, alpha], axis=1) * acc_sc[g, rows, :] + jnp.dot(
            p, v_aug, preferred_element_type=F32)

    def block(start, w, mask):
        start = pl.multiple_of(start, LANE)
        n = len(units)
        lead = FOX_SCORE_LEAD
        s = {k: score(units[k], start, w) for k in range(lead)}
        pa = {}
        for k in range(n):
            pa[k] = softmax(units[k], s.pop(k), mask)
            if k + lead < n:
                s[k + lead] = score(units[k + lead], start, w)
            if k >= 1:
                accumulate(units[k - 1], start, *pa.pop(k - 1))
        accumulate(units[n - 1], start, *pa.pop(n - 1))

    for g in heads:
        m_sc[g] = jnp.full((t, LANE), NEG_BIG, F32)
        acc_sc[g] = jnp.zeros((t, 2 * FOX_HD), F32)

    wide = FOX_KEY_TILES * t

    def wide_body(kb, _):
        block(kb * wide, wide, False)
        return 0

    def narrow_body(kb, _):
        block(kb * t, t, False)
        return 0

    n_wide = qi // FOX_KEY_TILES
    lax.fori_loop(0, n_wide, wide_body, 0)
    lax.fori_loop(n_wide * FOX_KEY_TILES, qi, narrow_body, 0)
    block(qi * t, t, True)
    for g in heads:
        acc = acc_sc[g]
        og = og_ref[:, cols[g]].astype(F32)
        o_ref[:, cols[g]] = (acc[:, :FOX_HD] / acc[:, FOX_HD:] * jax.nn.sigmoid(og)).astype(BF16)


def _fox_attention(q_og, kv, c_t):
    w = FOX_GROUP * FOX_HD
    n_groups = FOX_HEADS // FOX_GROUP
    return pl.pallas_call(
        _fox_kernel,
        grid=(n_groups, N_ROW_TILES),
        in_specs=[
            pl.BlockSpec((ROW_TILE, w), lambda hg, qi: (qi, hg)),
            pl.BlockSpec((L_PAD, w), lambda hg, qi: (0, hg), pipeline_mode=pl.Buffered(1)),
            pl.BlockSpec((L_PAD, w), lambda hg, qi: (0, n_groups + hg),
                         pipeline_mode=pl.Buffered(1)),
            pl.BlockSpec((ROW_TILE, w), lambda hg, qi: (qi, n_groups + hg)),
            pl.BlockSpec((FOX_GROUP, 1, L_PAD), lambda hg, qi: (hg, 0, 0)),
        ],
        out_specs=pl.BlockSpec((ROW_TILE, w), lambda hg, qi: (qi, hg)),
        out_shape=jax.ShapeDtypeStruct((L_PAD, D_MODEL), BF16),
        scratch_shapes=[pltpu.VMEM((FOX_GROUP, ROW_TILE, LANE), F32),
                        pltpu.VMEM((FOX_GROUP, ROW_TILE, 2 * FOX_HD), F32)],
        compiler_params=_cparams("parallel", "parallel"),
        name="fox_attention",
    )(q_og, kv, kv, q_og, c_t)


def _pad_cols(w, n):
    return jnp.pad(w, ((0, 0), (0, n - w.shape[1])))


def kernel(x, meta, ln_g, ln_b, gla_w_in, gla_w_g2, gla_b_g2, gla_norm_g, gla_w_out, kv_w, kv_bf,
           fox_w_in, fox_w_out, ffn_w_up, ffn_conv_w, ffn_conv_b, ffn_w_down):
    assert x.shape == (1, SEQ, D_MODEL)
    tail = jnp.zeros((L_PAD - L_VALID, D_MODEL), F32)
    hf = jnp.concatenate([meta.astype(F32), x[0], tail], axis=0)
    hb = hf.astype(BF16)

    ones = lambda n: jnp.ones((1, n), F32)
    row = lambda v: v.reshape(1, -1)
    g_lo, g_hi = 2 * GLA_DK + GLA_DV, 2 * GLA_DK + GLA_DV + GLA_RANK

    gla_in = gla_w_in[:, :, :g_lo].astype(BF16)
    gla_r = gla_w_in[:, :, g_hi:].astype(BF16)
    gla_g1 = jnp.pad(gla_w_in[:, :, g_lo:g_hi],
                     ((0, 0), (0, 0), (0, LANE - GLA_RANK))).astype(BF16)
    gla_g2 = jnp.pad(gla_w_g2, ((0, 0), (0, LANE - GLA_RANK), (0, 0))).astype(BF16)
    gla_out = gla_w_out.astype(BF16)
    kv_in = kv_w[:, :2 * D_MODEL].astype(BF16)[None]
    kv_f = _pad_cols(kv_w[:, 2 * D_MODEL:], LANE).astype(BF16)[None]
    fox_in = fox_w_in.astype(BF16)
    fox_out = fox_w_out.astype(BF16)
    ffn_up = ffn_w_up.astype(BF16)
    ffn_down = ffn_w_down.astype(BF16)
    ffn_cb = ffn_conv_b[:, None, :]
    gla_q_scale = jnp.concatenate([jnp.full((1, GLA_DK), GLA_HK ** -0.5, F32), ones(g_lo - GLA_DK)],
                                  axis=1)
    fox_q_scale = jnp.concatenate([jnp.full((1, D_MODEL), FOX_HD ** -0.5 * LOG2_E, F32),
                                   ones(D_MODEL)], axis=1)

    c_t = kv = None
    for l in range(DEPTH):
        if l < N_A_LAYERS:
            qkv = _matmul(hb, gla_in, l, g_lo, gla_q_scale, BF16, PROJ_TILE)
            r = _matmul(hb, gla_r, l, GLA_DV, ones(GLA_DV), BF16, PROJ_TILE)
            g_low = _matmul(hb, gla_g1, l, LANE, ones(LANE), F32, LANE)
            mix = _gla_core(qkv, r, g_low, gla_g2[l], row(gla_b_g2[l]), row(gla_norm_g[l]))
            w_out, j = gla_out, l
        else:
            j = l - N_A_LAYERS
            if j == 0:
                kv = _matmul(hb, kv_in, 0, 2 * D_MODEL, ones(2 * D_MODEL), BF16, PROJ_TILE)
                f_logit = _matmul(hb, kv_f, 0, LANE, ones(LANE), F32, LANE)
                c_t = _cum_gate(f_logit, _pad_cols(row(kv_bf), LANE))[:FOX_HEADS, None, :]
            q_og = _matmul(hb, fox_in, j, 2 * D_MODEL, fox_q_scale, BF16, PROJ_TILE)
            mix = _fox_attention(q_og, kv, c_t)
            w_out = fox_out
        hf, hb = _proj_residual_ln(mix, w_out, j, hf, row(ln_g[l, 0]), row(ln_b[l, 0]))
        hf, hb = _conv_ffn_ln(hb, hf, ffn_up, ffn_conv_w, ffn_cb, ffn_down, l,
                              row(ln_g[l, 1]), row(ln_b[l, 1]))
    return hf[N_META:L_VALID][None]
```

```python
import jax
import jax.numpy as jnp
import numpy as np
from jax import lax
from jax.experimental import pallas as pl
from jax.experimental.pallas import tpu as pltpu

F32 = jnp.float32
BF16 = jnp.bfloat16

D_MODEL = 2048
SEQ = 8192
DEPTH = 4
N_META = 16
N_A_LAYERS = DEPTH // 2
ALPHA = (2.0 * DEPTH) ** 0.25
LN_EPS = 1e-5

GLA_HEADS = 4
GLA_DK = D_MODEL // 2
GLA_DV = D_MODEL
GLA_HK = GLA_DK // GLA_HEADS
GLA_HV = GLA_DV // GLA_HEADS
GLA_RANK = 16
GLA_TAU = 16.0

FOX_HEADS = 16
FOX_HD = D_MODEL // FOX_HEADS

D_FF = 5632

LANE = 128
L_VALID = N_META + SEQ
ROW_TILE = 640
L_PAD = 13 * ROW_TILE
N_ROW_TILES = L_PAD // ROW_TILE

GLA_CHUNK = 64
PROJ_TILE = 1024
PROJ_ROWS = 13 * LANE
FF_TILE = 512
FF_SUB = 256
OUT_ROW_PIECES = 2
FF_ROW_PIECES = 2
N_FF_TILES = D_FF // FF_TILE
FOX_GROUP = 4
FOX_KEY_TILES = 2
FOX_SCORE_LEAD = 2
CONV_HALO = 8

VMEM_LIMIT = 56 * 1024 * 1024
NEG_BIG = -1e30
LOG2_E = 1.4426950408889634


def _cparams(*sem):
    return pltpu.CompilerParams(dimension_semantics=sem, vmem_limit_bytes=VMEM_LIMIT)


def _layer_norm(y, g, b):
    mu = jnp.mean(y, axis=-1, keepdims=True)
    d = y - mu
    var = jnp.mean(d * d, axis=-1, keepdims=True)
    return d * lax.rsqrt(var + LN_EPS) * g + b


def _log_sigmoid(z):
    return jnp.minimum(z, 0.0) - jnp.log1p(jnp.exp(-jnp.abs(z)))


def _split3(x):
    hi = x.astype(BF16)
    r1 = x - hi.astype(F32)
    mid = r1.astype(BF16)
    lo = (r1 - mid.astype(F32)).astype(BF16)
    return hi, mid, lo


def _mm_kernel(x_ref, w_ref, s_ref, o_ref):
    tn = o_ref.shape[1]
    acc = jnp.dot(x_ref[...], w_ref[...], preferred_element_type=F32)
    scale = s_ref[:, pl.ds(pl.multiple_of(pl.program_id(1) * tn, LANE), tn)]
    o_ref[...] = (acc * scale).astype(o_ref.dtype)


def _matmul(x, w, layer, n, col_scale, out_dtype, tn):
    k = x.shape[1]
    return pl.pallas_call(
        _mm_kernel,
        grid=(L_PAD // PROJ_ROWS, n // tn),
        in_specs=[
            pl.BlockSpec((PROJ_ROWS, k), lambda i, j: (i, 0)),
            pl.BlockSpec((None, k, tn), lambda i, j: (layer, 0, j)),
            pl.BlockSpec((1, n), lambda i, j: (0, 0)),
        ],
        out_specs=pl.BlockSpec((PROJ_ROWS, tn), lambda i, j: (i, j)),
        out_shape=jax.ShapeDtypeStruct((L_PAD, n), out_dtype),
        compiler_params=_cparams("parallel", "parallel"),
        name="proj_matmul",
    )(x, w, col_scale)


def _proj_ln_kernel(x_ref, w_ref, h_ref, g_ref, b_ref, of_ref, ob_ref):
    n = OUT_ROW_PIECES
    piece = ROW_TILE // n
    rows = [slice(r * piece, (r + 1) * piece) for r in range(n)]
    mm = lambda r: jnp.dot(x_ref[rows[r], :], w_ref[...], preferred_element_type=F32)
    acc = mm(0)
    for r in range(n):
        nxt = mm(r + 1) if r + 1 < n else None
        out = _layer_norm(ALPHA * h_ref[rows[r], :] + acc, g_ref[...], b_ref[...])
        of_ref[rows[r], :] = out
        ob_ref[rows[r], :] = out.astype(BF16)
        acc = nxt


def _proj_residual_ln(x, w, layer, h, g, b):
    row = lambda i: (i, 0)
    fixed = lambda i: (0, 0)
    return pl.pallas_call(
        _proj_ln_kernel,
        grid=(N_ROW_TILES,),
        in_specs=[
            pl.BlockSpec((ROW_TILE, D_MODEL), row),
            pl.BlockSpec((None, D_MODEL, D_MODEL), lambda i: (layer, 0, 0)),
            pl.BlockSpec((ROW_TILE, D_MODEL), row),
            pl.BlockSpec((1, D_MODEL), fixed),
            pl.BlockSpec((1, D_MODEL), fixed),
        ],
        out_specs=[pl.BlockSpec((ROW_TILE, D_MODEL), row), pl.BlockSpec((ROW_TILE, D_MODEL), row)],
        out_shape=[jax.ShapeDtypeStruct((L_PAD, D_MODEL), F32),
                   jax.ShapeDtypeStruct((L_PAD, D_MODEL), BF16)],
        compiler_params=_cparams("parallel"),
        name="out_proj_ln",
    )(x, w, h, g, b)


def _ffn_kernel(xb_ref, xf_ref, wu_ref, wg_ref, cw_ref, cb_ref, wd_ref,
                g_ref, b_ref, of_ref, ob_ref, ext_ref, carry_ref):
    i = pl.program_id(0)
    j = pl.program_id(1)

    @pl.when(i == 0)
    def _():
        carry_ref[j] = jnp.zeros(carry_ref.shape[1:], F32)

    @pl.when(j == 0)
    def _():
        of_ref[...] = jnp.zeros_like(of_ref)

    piece = ROW_TILE // FF_ROW_PIECES
    pieces = [slice(r * piece, (r + 1) * piece) for r in range(FF_ROW_PIECES)]
    n_chunks = FF_TILE // FF_SUB
    chunks = [slice(c * FF_SUB, (c + 1) * FF_SUB) for c in range(n_chunks)]

    def up(c, r):
        rows = pieces[r]
        for slot, w_ref in ((2 * c, wu_ref), (2 * c + 1, wg_ref)):
            hval = jnp.dot(xb_ref[rows, :], w_ref[:, chunks[c]], preferred_element_type=F32)
            if r == 0:
                ext_ref[slot, 0:CONV_HALO, :] = carry_ref[j, slot]
            if r == FF_ROW_PIECES - 1:
                carry_ref[j, slot] = hval[piece - CONV_HALO:, :]
            ext_ref[slot, CONV_HALO + rows.start:CONV_HALO + rows.stop, :] = hval

    def gate(c, r):
        def conv(slot, first_col):
            cols = pl.ds(pl.multiple_of(first_col, LANE), FF_SUB)
            cw = cw_ref[:, cols]
            lo = CONV_HALO + pieces[r].start
            return (cw[2:3, :] * ext_ref[slot, lo:lo + piece, :]
                    + cw[1:2, :] * ext_ref[slot, lo - 1:lo - 1 + piece, :]
                    + cw[0:1, :] * ext_ref[slot, lo - 2:lo - 2 + piece, :]
                    + cb_ref[:, cols])
        u = conv(2 * c, j * FF_TILE + c * FF_SUB)
        gt = conv(2 * c + 1, D_FF + j * FF_TILE + c * FF_SUB)
        return (gt * jax.nn.sigmoid(gt) * u).astype(BF16)

    for r in range(FF_ROW_PIECES):
        up(0, r)
    acts = [[None] * FF_ROW_PIECES for _ in range(n_chunks)]
    for c in range(n_chunks):
        for r in range(FF_ROW_PIECES):
            acts[c][r] = gate(c, r)
            if c + 1 < n_chunks:
                up(c + 1, r)
            else:
                rows = pieces[r]
                part = None
                for cc in range(n_chunks):
                    d = jnp.dot(acts[cc][r], wd_ref[chunks[cc], :], preferred_element_type=F32)
                    part = d if part is None else part + d
                of_ref[rows, :] += part

    @pl.when(j == N_FF_TILES - 1)
    def _():
        out = _layer_norm(ALPHA * xf_ref[...] + of_ref[...], g_ref[...], b_ref[...])
        of_ref[...] = out
        ob_ref[...] = out.astype(BF16)


def _conv_ffn_ln(hb, hf, w_up, conv_w, conv_b, w_down, layer, g, b):
    row = lambda i, j: (i, 0)
    fixed = lambda i, j: (0, 0)
    ucol = lambda i, j: (layer, 0, j)
    gcol = lambda i, j: (layer, 0, N_FF_TILES + j)
    n_slots = 2 * FF_TILE // FF_SUB
    return pl.pallas_call(
        _ffn_kernel,
        grid=(N_ROW_TILES, N_FF_TILES),
        in_specs=[
            pl.BlockSpec((ROW_TILE, D_MODEL), row),
            pl.BlockSpec((ROW_TILE, D_MODEL), row),
            pl.BlockSpec((None, D_MODEL, FF_TILE), ucol),
            pl.BlockSpec((None, D_MODEL, FF_TILE), gcol),
            pl.BlockSpec((None, 3, 2 * D_FF), lambda i, j: (layer, 0, 0)),
            pl.BlockSpec((None, 1, 2 * D_FF), lambda i, j: (layer, 0, 0)),
            pl.BlockSpec((None, FF_TILE, D_MODEL), lambda i, j: (layer, j, 0)),
            pl.BlockSpec((1, D_MODEL), fixed),
            pl.BlockSpec((1, D_MODEL), fixed),
        ],
        out_specs=[pl.BlockSpec((ROW_TILE, D_MODEL), row), pl.BlockSpec((ROW_TILE, D_MODEL), row)],
        out_shape=[jax.ShapeDtypeStruct((L_PAD, D_MODEL), F32),
                   jax.ShapeDtypeStruct((L_PAD, D_MODEL), BF16)],
        scratch_shapes=[
            pltpu.VMEM((n_slots, ROW_TILE + CONV_HALO, FF_SUB), F32),
            pltpu.VMEM((N_FF_TILES, n_slots, CONV_HALO, FF_SUB), F32),
        ],
        compiler_params=_cparams("arbitrary", "arbitrary"),
        name="conv_ffn_ln",
    )(hb, hf, w_up, w_up, conv_w, conv_b, w_down, g, b)


def _gla_tables():
    c = GLA_CHUNK
    i = np.arange(c)[:, None]
    t = np.arange(c)[None, :]
    sums = [t <= i, t > i]
    masks = [i == t]
    s = c // 2
    while s >= 1:
        blk = i // s
        boundary = blk * s
        sums.append((blk % 2 == 1) & (t > boundary) & (t <= i))
        sums.append((blk % 2 == 0) & (t > i) & (t <= boundary + s))
        masks.append((blk % 2 == 1) & (t // s == blk - 1))
        s //= 2
    sums = np.concatenate(sums, axis=0).astype(np.float32)
    masks = np.concatenate(masks, axis=0).astype(np.float32)
    return np.concatenate([sums, sums, sums], axis=1), masks


def _gla_kernel(q_ref, k_ref, v_ref, r_ref, gl_ref, w2_ref, b2_ref, ng_ref, sum_ref, mask_ref,
                o_ref, st_ref):
    c = GLA_CHUNK
    n_levels = mask_ref.shape[0] // c - 1
    nt = (((1,), (1,)), ((), ()))

    @pl.when(pl.program_id(0) == 0)
    def _():
        st_ref[...] = jnp.zeros_like(st_ref)

    z = jnp.dot(gl_ref[...].astype(BF16), w2_ref[...], preferred_element_type=F32) + b2_ref[...]
    log2_a = _log_sigmoid(z) * (LOG2_E / GLA_TAU)
    sum_table = sum_ref[...]
    heads = range(GLA_HEADS)
    kc = [slice(h * GLA_HK, (h + 1) * GLA_HK) for h in heads]
    vc = [slice(h * GLA_HV, (h + 1) * GLA_HV) for h in heads]
    decay = [jnp.exp2(jnp.dot(sum_table, jnp.concatenate(_split3(log2_a[:, kc[h]]), axis=0),
                              preferred_element_type=F32)) for h in heads]
    qf = [q_ref[:, kc[h]].astype(F32) for h in heads]
    kf = [k_ref[:, kc[h]].astype(F32) for h in heads]
    o = [lax.dot_general((qf[h] * decay[h][0:c]).astype(BF16), st_ref[h].astype(BF16), nt,
                         preferred_element_type=F32) for h in heads]
    att = [mask_ref[0:c, :] * lax.dot_general(q_ref[:, kc[h]], k_ref[:, kc[h]], nt,
                                              preferred_element_type=F32) for h in heads]
    for lv in range(n_levels):
        r0 = (2 + 2 * lv) * c
        for h in heads:
            q_lv = (qf[h] * decay[h][r0:r0 + c]).astype(BF16)
            k_lv = (kf[h] * decay[h][r0 + c:r0 + 2 * c]).astype(BF16)
            att[h] = att[h] + mask_ref[(lv + 1) * c:(lv + 2) * c, :] * lax.dot_general(
                q_lv, k_lv, nt, preferred_element_type=F32)
    for h in heads:
        o[h] = o[h] + jnp.dot(att[h].astype(BF16), v_ref[:, vc[h]], preferred_element_type=F32)
    for h in heads:
        st_ref[h] = st_ref[h] * decay[h][c - 1:c, :] + lax.dot_general(
            v_ref[:, vc[h]], (kf[h] * decay[h][c:2 * c]).astype(BF16), (((0,), (0,)), ((), ())),
            preferred_element_type=F32)
    for h in heads:
        y = o[h] * lax.rsqrt(jnp.mean(o[h] * o[h], axis=-1, keepdims=True) + LN_EPS) * ng_ref[...]
        rg = r_ref[:, vc[h]].astype(F32)
        o_ref[:, vc[h]] = (y * (rg * jax.nn.sigmoid(rg))).astype(BF16)


def _gla_core(qkv, r, g_low, w_g2, b_g2, norm_g):
    c = GLA_CHUNK
    sums, masks = _gla_tables()
    fixed = lambda n: (0, 0)
    return pl.pallas_call(
        _gla_kernel,
        grid=(L_PAD // c,),
        in_specs=[
            pl.BlockSpec((c, GLA_DK), lambda n: (n, 0)),
            pl.BlockSpec((c, GLA_DK), lambda n: (n, 1)),
            pl.BlockSpec((c, GLA_DV), lambda n: (n, 1)),
            pl.BlockSpec((c, GLA_DV), lambda n: (n, 0)),
            pl.BlockSpec((c, LANE), lambda n: (n, 0)),
            pl.BlockSpec((LANE, GLA_DK), fixed),
            pl.BlockSpec((1, GLA_DK), fixed),
            pl.BlockSpec((1, GLA_HV), fixed),
            pl.BlockSpec(sums.shape, fixed),
            pl.BlockSpec(masks.shape, fixed),
        ],
        out_specs=pl.BlockSpec((c, GLA_DV), lambda n: (n, 0)),
        out_shape=jax.ShapeDtypeStruct((L_PAD, GLA_DV), BF16),
        scratch_shapes=[pltpu.VMEM((GLA_HEADS, GLA_HV, GLA_HK), F32)],
        compiler_params=_cparams("arbitrary"),
        name="gla_core",
    )(qkv, qkv, qkv, r, g_low, w_g2, b_g2, norm_g,
      jnp.asarray(sums, BF16), jnp.asarray(masks, F32))


def _cum_gate_kernel(fl_ref, bf_ref, o_ref, carry_ref):
    t = ROW_TILE

    @pl.when(pl.program_id(0) == 0)
    def _():
        carry_ref[...] = jnp.zeros_like(carry_ref)

    log_f = _log_sigmoid(fl_ref[...] + bf_ref[...]) * LOG2_E
    rid = lax.broadcasted_iota(jnp.int32, (t, t), 0)
    cid = lax.broadcasted_iota(jnp.int32, (t, t), 1)
    upper = jnp.where(rid <= cid, 1.0, 0.0).astype(BF16)
    cum = sum(lax.dot_general(p, upper, (((0,), (0,)), ((), ())), preferred_element_type=F32)
              for p in _split3(log_f))
    cum = cum + carry_ref[...]
    o_ref[...] = cum
    carry_ref[...] = cum[:, t - 1:t]


def _cum_gate(f_logit, bias):
    return pl.pallas_call(
        _cum_gate_kernel,
        grid=(N_ROW_TILES,),
        in_specs=[pl.BlockSpec((ROW_TILE, LANE), lambda i: (i, 0)),
                  pl.BlockSpec((1, LANE), lambda i: (0, 0))],
        out_specs=pl.BlockSpec((LANE, ROW_TILE), lambda i: (0, i)),
        out_shape=jax.ShapeDtypeStruct((LANE, L_PAD), F32),
        scratch_shapes=[pltpu.VMEM((LANE, 1), F32)],
        compiler_params=_cparams("arbitrary"),
        name="cum_forget_gate",
    )(f_logit, bias)


def _fox_kernel(q_ref, k_ref, v_ref, og_ref, c_ref, o_ref, m_sc, acc_sc):
    t = ROW_TILE
    qi = pl.program_id(1)
    rid = lax.broadcasted_iota(jnp.int32, (t, t), 0)
    cid = lax.broadcasted_iota(jnp.int32, (t, t), 1)
    causal = cid <= rid
    heads = range(FOX_GROUP)
    cols = [slice(g * FOX_HD, (g + 1) * FOX_HD) for g in heads]
    q_start = pl.multiple_of(qi * t, LANE)
    c_q0 = [c_ref[g, :, pl.ds(q_start, LANE)][:, 0:1] for g in heads]

    half = t // 2
    units = [(g, slice(r * half, (r + 1) * half)) for g in heads for r in range(2)]

    def score(unit, start, w):
        g, rows = unit
        return (lax.dot_general(q_ref[rows, cols[g]], k_ref[pl.ds(start, w), cols[g]],
                                (((1,), (1,)), ((), ())), preferred_element_type=F32)
                + (c_q0[g] - c_ref[g, :, pl.ds(start, w)]))

    def softmax(unit, s, mask):
        g, rows = unit
        if mask:
            s = jnp.where(causal[rows, :], s, NEG_BIG)
        m = m_sc[g, rows, :]
        m_new = jnp.maximum(m, jnp.max(s, axis=-1, keepdims=True))
        m_sc[g, rows, :] = m_new
        p = jnp.exp2(s - jnp.concatenate([m_new] * (s.shape[1] // LANE), axis=1)).astype(BF16)
        return p, jnp.exp2(m - m_new)

    def accumulate(unit, start, p, alpha):
        g, rows = unit
        w = p.shape[1]
        v_aug = jnp.concatenate([v_ref[pl.ds(start, w), cols[g]], jnp.ones((w, FOX_HD), BF16)],
                                axis=1)
        acc_sc[g, rows, :] = jnp.concatenate([alpha, alpha], axis=1) * acc_sc[g, rows, :] + jnp.dot(
            p, v_aug, preferred_element_type=F32)

    def block(start, w, mask):
        start = pl.multiple_of(start, LANE)
        n = len(units)
        lead = FOX_SCORE_LEAD
        s = {k: score(units[k], start, w) for k in range(lead)}
        pa = {}
        for k in range(n):
            pa[k] = softmax(units[k], s.pop(k), mask)
            if k + lead < n:
                s[k + lead] = score(units[k + lead], start, w)
            if k >= 1:
                accumulate(units[k - 1], start, *pa.pop(k - 1))
        accumulate(units[n - 1], start, *pa.pop(n - 1))

    for g in heads:
        m_sc[g] = jnp.full((t, LANE), NEG_BIG, F32)
        acc_sc[g] = jnp.zeros((t, 2 * FOX_HD), F32)

    wide = FOX_KEY_TILES * t

    def wide_body(kb, _):
        block(kb * wide, wide, False)
        return 0

    def narrow_body(kb, _):
        block(kb * t, t, False)
        return 0

    n_wide = qi // FOX_KEY_TILES
    lax.fori_loop(0, n_wide, wide_body, 0)
    lax.fori_loop(n_wide * FOX_KEY_TILES, qi, narrow_body, 0)
    block(qi * t, t, True)
    for g in heads:
        acc = acc_sc[g]
        og = og_ref[:, cols[g]].astype(F32)
        o_ref[:, cols[g]] = (acc[:, :FOX_HD] / acc[:, FOX_HD:] * jax.nn.sigmoid(og)).astype(BF16)


def _fox_attention(q_og, kv, c_t):
    w = FOX_GROUP * FOX_HD
    n_groups = FOX_HEADS // FOX_GROUP
    return pl.pallas_call(
        _fox_kernel,
        grid=(n_groups, N_ROW_TILES),
        in_specs=[
            pl.BlockSpec((ROW_TILE, w), lambda hg, qi: (qi, hg)),
            pl.BlockSpec((L_PAD, w), lambda hg, qi: (0, hg), pipeline_mode=pl.Buffered(1)),
            pl.BlockSpec((L_PAD, w), lambda hg, qi: (0, n_groups + hg),
                         pipeline_mode=pl.Buffered(1)),
            pl.BlockSpec((ROW_TILE, w), lambda hg, qi: (qi, n_groups + hg)),
            pl.BlockSpec((FOX_GROUP, 1, L_PAD), lambda hg, qi: (hg, 0, 0)),
        ],
        out_specs=pl.BlockSpec((ROW_TILE, w), lambda hg, qi: (qi, hg)),
        out_shape=jax.ShapeDtypeStruct((L_PAD, D_MODEL), BF16),
        scratch_shapes=[pltpu.VMEM((FOX_GROUP, ROW_TILE, LANE), F32),
                        pltpu.VMEM((FOX_GROUP, ROW_TILE, 2 * FOX_HD), F32)],
        compiler_params=_cparams("parallel", "parallel"),
        name="fox_attention",
    )(q_og, kv, kv, q_og, c_t)


def _pad_cols(w, n):
    return jnp.pad(w, ((0, 0), (0, n - w.shape[1])))


def kernel(x, meta, ln_g, ln_b, gla_w_in, gla_w_g2, gla_b_g2, gla_norm_g, gla_w_out, kv_w, kv_bf,
           fox_w_in, fox_w_out, ffn_w_up, ffn_conv_w, ffn_conv_b, ffn_w_down):
    assert x.shape == (1, SEQ, D_MODEL)
    tail = jnp.zeros((L_PAD - L_VALID, D_MODEL), F32)
    hf = jnp.concatenate([meta.astype(F32), x[0], tail], axis=0)
    hb = hf.astype(BF16)

    ones = lambda n: jnp.ones((1, n), F32)
    row = lambda v: v.reshape(1, -1)
    g_lo, g_hi = 2 * GLA_DK + GLA_DV, 2 * GLA_DK + GLA_DV + GLA_RANK

    gla_in = gla_w_in[:, :, :g_lo].astype(BF16)
    gla_r = gla_w_in[:, :, g_hi:].astype(BF16)
    gla_g1 = jnp.pad(gla_w_in[:, :, g_lo:g_hi],
                     ((0, 0), (0, 0), (0, LANE - GLA_RANK))).astype(BF16)
    gla_g2 = jnp.pad(gla_w_g2, ((0, 0), (0, LANE - GLA_RANK), (0, 0))).astype(BF16)
    gla_out = gla_w_out.astype(BF16)
    kv_in = kv_w[:, :2 * D_MODEL].astype(BF16)[None]
    kv_f = _pad_cols(kv_w[:, 2 * D_MODEL:], LANE).astype(BF16)[None]
    fox_in = fox_w_in.astype(BF16)
    fox_out = fox_w_out.astype(BF16)
    ffn_up = ffn_w_up.astype(BF16)
    ffn_down = ffn_w_down.astype(BF16)
    ffn_cb = ffn_conv_b[:, None, :]
    gla_q_scale = jnp.concatenate([jnp.full((1, GLA_DK), GLA_HK ** -0.5, F32), ones(g_lo - GLA_DK)],
                                  axis=1)
    fox_q_scale = jnp.concatenate([jnp.full((1, D_MODEL), FOX_HD ** -0.5 * LOG2_E, F32),
                                   ones(D_MODEL)], axis=1)

    c_t = kv = None
    for l in range(DEPTH):
        if l < N_A_LAYERS:
            qkv = _matmul(hb, gla_in, l, g_lo, gla_q_scale, BF16, PROJ_TILE)
            r = _matmul(hb, gla_r, l, GLA_DV, ones(GLA_DV), BF16, PROJ_TILE)
            g_low = _matmul(hb, gla_g1, l, LANE, ones(LANE), F32, LANE)
            mix = _gla_core(qkv, r, g_low, gla_g2[l], row(gla_b_g2[l]), row(gla_norm_g[l]))
            w_out, j = gla_out, l
        else:
            j = l - N_A_LAYERS
            if j == 0:
                kv = _matmul(hb, kv_in, 0, 2 * D_MODEL, ones(2 * D_MODEL), BF16, PROJ_TILE)
                f_logit = _matmul(hb, kv_f, 0, LANE, ones(LANE), F32, LANE)
                c_t = _cum_gate(f_logit, _pad_cols(row(kv_bf), LANE))[:FOX_HEADS, None, :]
            q_og = _matmul(hb, fox_in, j, 2 * D_MODEL, fox_q_scale, BF16, PROJ_TILE)
            mix = _fox_attention(q_og, kv, c_t)
            w_out = fox_out
        hf, hb = _proj_residual_ln(mix, w_out, j, hf, row(ln_g[l, 0]), row(ln_b[l, 0]))
        hf, hb = _conv_ffn_ln(hb, hf, ffn_up, ffn_conv_w, ffn_cb, ffn_down, l,
                              row(ln_g[l, 1]), row(ln_b[l, 1]))
    return hf[N_META:L_VALID][None]
```
